```python
import math
import jax, jax.numpy as jnp
from jax import lax
import numpy as np

D_MODEL = 1024
BATCH = 16
SEQ = 2048
DEPTH = 1
DEC_BATCH = 32
DEC_SEQ = 64
PAST_LEN = 4096

CHUNK = 64
QBLOCK = 128
N_HEADS = 4
D_QK = 64
D_V = 2 * D_QK
D_ATT = N_HEADS * D_V
N_GROUPS = 4
GROUP_DIM = 128
D_GMLP = N_GROUPS * GROUP_DIM
GMLP_CHUNK = 128
D_MIX = D_ATT + D_GMLP
D_IN = 3 * D_ATT + 2 * D_GMLP
D_FF = 4 * D_MODEL
NUM_BUCKETS = 32
MAX_DISTANCE = 1024
ALPHA = (2.0 * DEPTH) ** 0.25
BETA = (8.0 * DEPTH) ** -0.25
LN_EPS = 1e-5
NEG_INF = -1e30

kernel_name = 'hymba_diffattn_gmlp_streaming_step'


def _lambda_init(layer):
    return 0.8 - 0.6 * math.exp(-0.3 * layer)


def _layer_norm(x, g, b):
    xf = x.astype(jnp.float32)
    mu = jnp.mean(xf, axis=-1, keepdims=True)
    var = jnp.mean(jnp.square(xf - mu), axis=-1, keepdims=True)
    y = (xf - mu) * lax.rsqrt(var + LN_EPS) * g.astype(jnp.float32) + b.astype(jnp.float32)
    return y.astype(x.dtype)


def _rms_norm(x, g):
    xf = x.astype(jnp.float32)
    y = xf * lax.rsqrt(jnp.mean(jnp.square(xf), axis=-1, keepdims=True) + LN_EPS) * g.astype(jnp.float32)
    return y.astype(x.dtype)


def _rel_bucket(rel):
    half = NUM_BUCKETS // 2
    max_exact = half // 2
    n = jnp.abs(rel)
    offset = jnp.where(rel > 0, half, 0)
    nf = jnp.maximum(n, 1).astype(jnp.float32)
    large = max_exact + (jnp.log(nf / max_exact) / math.log(MAX_DISTANCE / max_exact)
                         * (half - max_exact)).astype(jnp.int32)
    large = jnp.minimum(large, half - 1)
    return offset + jnp.where(n < max_exact, n, large)


def _diff_lambda(lq1, lk1, lq2, lk2, lam_init):
    f32 = jnp.float32
    return (jnp.exp(jnp.sum(lq1.astype(f32) * lk1.astype(f32)))
            - jnp.exp(jnp.sum(lq2.astype(f32) * lk2.astype(f32))) + lam_init)


def _diff_attention(q, k, v, q_pos, k_pos, rel_table, lam, subln_gain, lam_init):
    b, tq = q.shape[0], q.shape[1]
    tk = k.shape[1]
    qh = q.reshape(b, tq, N_HEADS, 2, D_QK)
    kh = k.reshape(b, tk, N_HEADS, 2, D_QK)
    s = jnp.einsum('bqhcd,bkhcd->bchqk', qh, kh).astype(jnp.float32) * (D_QK ** -0.5)
    rel = k_pos[None, :] - q_pos[:, None]
    bias = jnp.transpose(rel_table[_rel_bucket(rel)].astype(jnp.float32), (2, 0, 1))
    visible = (k_pos[None, :] // CHUNK) <= (q_pos[:, None] // CHUNK)
    s = jnp.where(visible, s + bias, NEG_INF)
    p = jax.nn.softmax(s, axis=-1)
    a = p[:, 0] - lam * p[:, 1]
    o = jnp.einsum('bhqk,bkhe->bqhe', a.astype(v.dtype), v)
    return _rms_norm(o, subln_gain) * (1.0 - lam_init)


def _project(x, w_in):
    b, t, _ = x.shape
    h = jnp.einsum('btd,df->btf', x, w_in)
    q = h[..., :D_ATT].reshape(b, t, N_HEADS, D_V)
    k = h[..., D_ATT:2 * D_ATT].reshape(b, t, N_HEADS, D_V)
    v = h[..., 2 * D_ATT:3 * D_ATT].reshape(b, t, N_HEADS, D_V)
    z = jax.nn.gelu(h[..., 3 * D_ATT:])
    u = z[..., :D_GMLP].reshape(b, t, N_GROUPS, GROUP_DIM)
    g = z[..., D_GMLP:].reshape(b, t, N_GROUPS, GROUP_DIM)
    return q, k, v, u, g


def _spatial_weights(w_spatial):
    i = jnp.arange(GMLP_CHUNK)
    mask = (i[:, None] // CHUNK) >= (i[None, :] // CHUNK)
    return jnp.where(mask[None], w_spatial, jnp.zeros_like(w_spatial))


def _finish(x, attn, gm, w_out, ln1_g, ln1_b, w_up, w_down, ln2_g, ln2_b):
    b, t, _ = x.shape
    mixed = jnp.concatenate([attn.reshape(b, t, D_ATT), gm.reshape(b, t, D_GMLP)], axis=-1)
    x1 = _layer_norm(ALPHA * x + mixed @ w_out, ln1_g, ln1_b)
    f = jnp.square(jax.nn.relu(x1 @ w_up)) @ w_down
    return _layer_norm(ALPHA * x1 + f, ln2_g, ln2_b)


def setup_inputs(seed: int = 0) -> dict:
    key = jax.random.key(seed)
    ks = jax.random.split(key, 24)
    n = jax.random.normal
    f32 = jnp.float32
    return {
        'x_prompt': n(ks[0], (BATCH, SEQ, D_MODEL), f32),
        'x_sample': n(ks[1], (DEC_BATCH, DEC_SEQ, D_MODEL), f32),
        'cache_k': n(ks[2], (DEPTH, DEC_BATCH, PAST_LEN, N_HEADS, D_V), f32),
        'cache_v': n(ks[3], (DEPTH, DEC_BATCH, PAST_LEN, N_HEADS, D_V), f32),
        'rel_bias_table': 0.1 * n(ks[4], (NUM_BUCKETS, N_HEADS), f32),
        'w_in': n(ks[5], (DEPTH, D_MODEL, D_IN), f32) * D_MODEL ** -0.5,
        'lambda_q1': 0.1 * n(ks[6], (DEPTH, D_QK), f32),
        'lambda_k1': 0.1 * n(ks[7], (DEPTH, D_QK), f32),
        'lambda_q2': 0.1 * n(ks[8], (DEPTH, D_QK), f32),
        'lambda_k2': 0.1 * n(ks[9], (DEPTH, D_QK), f32),
        'subln_gain': 1.0 + 0.01 * n(ks[10], (DEPTH, N_HEADS, D_V), f32),
        'gmlp_ln_gain': 1.0 + 0.01 * n(ks[11], (DEPTH, N_GROUPS, GROUP_DIM), f32),
        'gmlp_ln_bias': 0.01 * n(ks[12], (DEPTH, N_GROUPS, GROUP_DIM), f32),
        'w_spatial': n(ks[13], (DEPTH, N_GROUPS, GMLP_CHUNK, GMLP_CHUNK), f32) * GMLP_CHUNK ** -0.5,
        'b_spatial': 1.0 + 0.01 * n(ks[14], (DEPTH, N_GROUPS, GMLP_CHUNK), f32),
        'w_out': n(ks[15], (DEPTH, D_MIX, D_MODEL), f32) * (D_MIX ** -0.5 * BETA),
        'ln1_gain': 1.0 + 0.01 * n(ks[16], (DEPTH, D_MODEL), f32),
        'ln1_bias': 0.01 * n(ks[17], (DEPTH, D_MODEL), f32),
        'w_ffn_up': n(ks[18], (DEPTH, D_MODEL, D_FF), f32) * D_MODEL ** -0.5,
        'w_ffn_down': n(ks[19], (DEPTH, D_FF, D_MODEL), f32) * (D_FF ** -0.5 * BETA),
        'ln2_gain': 1.0 + 0.01 * n(ks[20], (DEPTH, D_MODEL), f32),
        'ln2_bias': 0.01 * n(ks[21], (DEPTH, D_MODEL), f32),
    }


def reference(x_prompt, x_sample, cache_k, cache_v, rel_bias_table, w_in,
              lambda_q1, lambda_k1, lambda_q2, lambda_k2, subln_gain,
              gmlp_ln_gain, gmlp_ln_bias, w_spatial, b_spatial, w_out,
              ln1_gain, ln1_bias, w_ffn_up, w_ffn_down, ln2_gain, ln2_bias):
    xp, xs = x_prompt, x_sample
    kp_list, vp_list, ks_list, vs_list, gs_list = [], [], [], [], []
    for l in range(DEPTH):
        lam_init = _lambda_init(l)
        lam = _diff_lambda(lambda_q1[l], lambda_k1[l], lambda_q2[l], lambda_k2[l], lam_init)
        ws = _spatial_weights(w_spatial[l])
        ffn_args = (w_out[l], ln1_gain[l], ln1_bias[l], w_ffn_up[l], w_ffn_down[l],
                    ln2_gain[l], ln2_bias[l])

        q, k, v, u, g = _project(xp, w_in[l])
        b, t = xp.shape[0], xp.shape[1]
        nb = t // QBLOCK
        qb = q.reshape(b, nb, QBLOCK, N_HEADS, D_V).transpose(1, 0, 2, 3, 4)
        starts = jnp.arange(nb, dtype=jnp.int32) * QBLOCK
        k_pos = jnp.arange(t, dtype=jnp.int32)

        def one_block(args, k=k, v=v, k_pos=k_pos, lam=lam, l=l, lam_init=lam_init):
            q_blk, s0 = args
            q_pos = s0 + jnp.arange(QBLOCK, dtype=jnp.int32)
            return _diff_attention(q_blk, k, v, q_pos, k_pos, rel_bias_table, lam,
                                   subln_gain[l], lam_init)

        attn = lax.map(one_block, (qb, starts)).transpose(1, 0, 2, 3, 4).reshape(b, t, N_HEADS, D_V)
        gn = _layer_norm(g, gmlp_ln_gain[l], gmlp_ln_bias[l])
        gb = gn.reshape(b, t // GMLP_CHUNK, GMLP_CHUNK, N_GROUPS, GROUP_DIM)
        sp = jnp.einsum('gij,bnjgc->bnigc', ws, gb) + b_spatial[l].T[:, :, None]
        gm = u * sp.reshape(b, t, N_GROUPS, GROUP_DIM)
        xp = _finish(xp, attn, gm, *ffn_args)
        kp_list.append(k)
        vp_list.append(v)

        q, k, v, u, g = _project(xs, w_in[l])
        t = xs.shape[1]
        past = cache_k.shape[2]
        k_all = jnp.concatenate([cache_k[l], k], axis=1)
        v_all = jnp.concatenate([cache_v[l], v], axis=1)
        q_pos = past + jnp.arange(t, dtype=jnp.int32)
        k_pos = jnp.arange(past + t, dtype=jnp.int32)
        attn = _diff_attention(q, k_all, v_all, q_pos, k_pos, rel_bias_table, lam,
                               subln_gain[l], lam_init)
        gn = _layer_norm(g, gmlp_ln_gain[l], gmlp_ln_bias[l])
        sp = jnp.einsum('gij,bjgc->bigc', ws[:, :t, :t], gn) + b_spatial[l][:, :t].T[:, :, None]
        gm = u * sp
        xs = _finish(xs, attn, gm, *ffn_args)
        ks_list.append(k)
        vs_list.append(v)
        gs_list.append(gn)

    new_k_prompt = jnp.stack(kp_list, axis=0)
    new_v_prompt = jnp.stack(vp_list, axis=0)
    new_k_sample = jnp.stack(ks_list, axis=0)
    new_v_sample = jnp.stack(vs_list, axis=0)
    new_gmlp_v_sample = jnp.stack(gs_list, axis=0)
    return (xp, xs, new_k_prompt, new_v_prompt, new_k_sample, new_v_sample, new_gmlp_v_sample)
```

```python
import functools
import math

import numpy as np
import jax
import jax.numpy as jnp
from jax import lax
from jax.experimental import pallas as pl
from jax.experimental.pallas import tpu as pltpu

F32 = jnp.float32
BF16 = jnp.bfloat16

CHUNK = 64
N_HEADS = 4
D_QK = 64
D_V = 2 * D_QK
D_ATT = N_HEADS * D_V
N_GROUPS = 4
GROUP_DIM = 128
D_GMLP = N_GROUPS * GROUP_DIM
GMLP_CHUNK = 128
NUM_BUCKETS = 32
MAX_DISTANCE = 1024
LN_EPS = 1e-5
NEG_INF = -1e30
QK_SCALE = D_QK ** -0.5

ROW_TILE = 512
ATT_TILE = 256
CACHE_TILE = 1024
FFN_CHUNK = 1024
VMEM_LIMIT = 56 * 1024 * 1024


def _lambda_init(layer):
    return 0.8 - 0.6 * math.exp(-0.3 * layer)


def _bucket_of_distance(max_n):
    half = NUM_BUCKETS // 2
    max_exact = half // 2
    n = np.arange(max_n + 1)

    def table(dtype):
        nf = np.maximum(n, 1).astype(dtype)
        scaled = np.log(nf / dtype(max_exact)) / dtype(math.log(MAX_DISTANCE / max_exact)) * dtype(half - max_exact)
        large = np.minimum(max_exact + scaled.astype(np.int64), half - 1)
        return np.where(n < max_exact, n, large)

    t64 = table(np.float64)
    assert np.array_equal(t64, table(np.float32)), "bucket boundaries are precision sensitive"
    return t64


def _rel_bucket_static(rel, bucket_table):
    return (NUM_BUCKETS // 2 if rel > 0 else 0) + int(bucket_table[abs(rel)])


def _bias_tile(tab_ref, head, d0, nrows, ncols, bucket_table):
    row = lax.broadcasted_iota(jnp.int32, (nrows, ncols), 0)
    col = lax.broadcasted_iota(jnp.int32, (nrows, ncols), 1)
    d = (col - row) + d0
    dmin, dmax = d0 - (nrows - 1), d0 + ncols - 1
    prev = _rel_bucket_static(dmin, bucket_table)
    val = jnp.full((nrows, ncols), tab_ref[prev, head], F32)
    for dd in range(dmin + 1, dmax + 1):
        b = _rel_bucket_static(dd, bucket_table)
        if b != prev:
            val = jnp.where(d >= dd, tab_ref[b, head], val)
            prev = b
    return val


def _bias_kernel(tab_ref, bp_ref, bs_ref, bn_ref, *, n_prompt, t, n_sample, tc, past, ts, bucket_table):
    row = lax.broadcasted_iota(jnp.int32, (t, t), 0)
    col = lax.broadcasted_iota(jnp.int32, (t, t), 1)
    visible = (col // CHUNK) <= (row // CHUNK)
    n_cache_tiles = past // tc
    for h in range(N_HEADS):
        for dl in range(n_prompt):
            tile = _bias_tile(tab_ref, h, -dl * t, t, t, bucket_table)
            if dl == 0:
                tile = jnp.where(visible, tile, NEG_INF)
            bp_ref[h, dl] = tile
        for i in range(n_sample):
            j = n_cache_tiles - n_sample + i
            bs_ref[h, i] = _bias_tile(tab_ref, h, j * tc - past, ts, tc, bucket_table)
        bn_ref[h] = _bias_tile(tab_ref, h, 0, ts, ts, bucket_table)


def _far_distance(bucket_table):
    n = len(bucket_table) - 1
    while n > 0 and bucket_table[n - 1] == bucket_table[-1]:
        n -= 1
    return n


def _num_varying_tiles(tile, bucket_table):
    far = _far_distance(bucket_table)
    n = 1
    while (n - 1) * tile + 1 < far:
        n += 1
    return n


def _gelu_tanh(x):
    c = math.sqrt(2.0 / math.pi)
    return 0.5 * x * (1.0 + jnp.tanh(c * (x + 0.044715 * (x * x * x))))


def _layer_norm(x, g, b):
    mu = jnp.mean(x, axis=-1, keepdims=True)
    xc = x - mu
    var = jnp.mean(xc * xc, axis=-1, keepdims=True)
    return xc * lax.rsqrt(var + LN_EPS) * g + b


def _proj_kernel(x_ref, w_ref, lng_ref, lnb_ref, ws_ref, bsp_ref,
                 q_ref, kf_ref, vf_ref, kb_ref, vb_ref, gm_ref, *gn_out, chunk):
    tm = x_ref.shape[0]
    xb = x_ref[...].astype(BF16)

    def seg(lo, hi):
        return jnp.dot(xb, w_ref[:, lo:hi], preferred_element_type=F32)

    q_ref[...] = (seg(0, D_ATT) * QK_SCALE).astype(BF16)
    k = seg(D_ATT, 2 * D_ATT)
    kf_ref[...] = k
    kb_ref[...] = k.astype(BF16)
    v = seg(2 * D_ATT, 3 * D_ATT)
    vf_ref[...] = v
    vb_ref[...] = v.astype(BF16)
    u = _gelu_tanh(seg(3 * D_ATT, 3 * D_ATT + D_GMLP))
    g = _gelu_tanh(seg(3 * D_ATT + D_GMLP, 3 * D_ATT + 2 * D_GMLP))

    lng = lng_ref[...]
    lnb = lnb_ref[...]
    gn = jnp.concatenate(
        [_layer_norm(g[:, i * GROUP_DIM:(i + 1) * GROUP_DIM],
                     lng[:, i * GROUP_DIM:(i + 1) * GROUP_DIM],
                     lnb[:, i * GROUP_DIM:(i + 1) * GROUP_DIM]) for i in range(N_GROUPS)], axis=1)
    if gn_out:
        gn_out[0][...] = gn
    gnb = gn.astype(BF16)

    ri = lax.broadcasted_iota(jnp.int32, (chunk, chunk), 0)
    ci = lax.broadcasted_iota(jnp.int32, (chunk, chunk), 1)
    keep = (ri // CHUNK) >= (ci // CHUNK)
    ws = [jnp.where(keep, ws_ref[i, :chunk, :chunk], 0.0).astype(BF16) for i in range(N_GROUPS)]
    bsp = bsp_ref[...]
    for c in range(tm // chunk):
        r0 = c * chunk
        sp = jnp.concatenate(
            [jnp.dot(ws[i], gnb[r0:r0 + chunk, i * GROUP_DIM:(i + 1) * GROUP_DIM],
                     preferred_element_type=F32) for i in range(N_GROUPS)], axis=1)
        gm_ref[r0:r0 + chunk, :] = (u[r0:r0 + chunk, :] * (sp + bsp)).astype(BF16)


def _diff_lambda(lamv_ref, lam_init):
    lv = lamv_ref[...]
    s1 = jnp.sum(lv[0:1, :] * lv[1:2, :], axis=-1, keepdims=True)
    s2 = jnp.sum(lv[2:3, :] * lv[3:4, :], axis=-1, keepdims=True)
    return jnp.exp(s1) - jnp.exp(s2) + lam_init


def _stack_halves(qh):
    lane = lax.broadcasted_iota(jnp.int32, qh.shape, 1)
    qf = qh.astype(F32)
    first = jnp.where(lane < D_QK, qf, 0.0)
    second = jnp.where(lane >= D_QK, qf, 0.0)
    return jnp.concatenate([first, second], axis=0).astype(BF16)


def _softmax_step(s, m, l, acc, vt):
    m_new = jnp.maximum(m, jnp.max(s, axis=-1, keepdims=True))
    alpha = jnp.exp(m - m_new)
    p = jnp.exp(s - m_new)
    l_new = alpha * l + jnp.sum(p, axis=-1, keepdims=True)
    acc_new = alpha * acc + jnp.dot(p.astype(BF16), vt, preferred_element_type=F32)
    return m_new, l_new, acc_new


def _diff_finish(acc, l, lam, gain, lam_init, t):
    o = acc / l
    od = o[:t] - lam * o[t:]
    ms = jnp.mean(od * od, axis=-1, keepdims=True)
    return od * lax.rsqrt(ms + LN_EPS) * gain * (1.0 - lam_init)


_NT = (((1,), (1,)), ((), ()))


def _attn_prompt_kernel(lamv_ref, gain_ref, q_ref, k_ref, v_ref, bias_ref, o_ref, *, t, n_bias, lam_init):
    qi = pl.program_id(1)
    lam = _diff_lambda(lamv_ref, lam_init)
    for h in range(N_HEADS):
        hs = slice(h * D_V, (h + 1) * D_V)
        qz = _stack_halves(q_ref[:, hs])

        def body(kj, carry, hs=hs, qz=qz, h=h):
            m, l, acc = carry
            off = pl.multiple_of(kj * t, t)
            kt = k_ref[pl.ds(off, t), hs]
            vt = v_ref[pl.ds(off, t), hs]
            s = lax.dot_general(qz, kt, _NT, preferred_element_type=F32)
            bias = bias_ref[h, jnp.minimum(qi - kj, n_bias - 1)]
            s = s + jnp.concatenate([bias, bias], axis=0)
            return _softmax_step(s, m, l, acc, vt)

        init = (jnp.full((2 * t, 1), -jnp.inf, F32), jnp.zeros((2 * t, 1), F32), jnp.zeros((2 * t, D_V), F32))
        _, l, acc = lax.fori_loop(0, qi + 1, body, init)
        o_ref[:, hs] = _diff_finish(acc, l, lam, gain_ref[h:h + 1, :], lam_init, t).astype(BF16)


def _attn_sample_kernel(lamv_ref, gain_ref, q_ref, kn_ref, vn_ref, kc_ref, vc_ref, bias_ref, biasn_ref,
                        o_ref, m_ref, l_ref, acc_ref, *, n_varying, lam_init):
    j = pl.program_id(1)
    nj = pl.num_programs(1)
    ts = q_ref.shape[0]
    tc = kc_ref.shape[0]

    @pl.when(j == 0)
    def _():
        m_ref[...] = jnp.full(m_ref.shape, -jnp.inf, F32)
        l_ref[...] = jnp.zeros(l_ref.shape, F32)
        acc_ref[...] = jnp.zeros(acc_ref.shape, F32)

    bi = jnp.maximum(j - (nj - n_varying - 1), 0)
    for h in range(N_HEADS):
        hs = slice(h * D_V, (h + 1) * D_V)
        qz = _stack_halves(q_ref[:, hs])
        kh = kc_ref[:, h, :].astype(BF16)
        vh = vc_ref[:, h, :].astype(BF16)
        s = lax.dot_general(qz, kh, _NT, preferred_element_type=F32)
        bias = bias_ref[h, bi]
        s = s + jnp.concatenate([bias, bias], axis=0)
        m, l, acc = _softmax_step(s, m_ref[h], l_ref[h], acc_ref[h], vh)
        m_ref[h] = m
        l_ref[h] = l
        acc_ref[h] = acc

    @pl.when(j == nj - 1)
    def _():
        lam = _diff_lambda(lamv_ref, lam_init)
        for h in range(N_HEADS):
            hs = slice(h * D_V, (h + 1) * D_V)
            qz = _stack_halves(q_ref[:, hs])
            s = lax.dot_general(qz, kn_ref[:, hs], _NT, preferred_element_type=F32)
            bias = biasn_ref[h]
            s = s + jnp.concatenate([bias, bias], axis=0)
            _, l, acc = _softmax_step(s, m_ref[h], l_ref[h], acc_ref[h], vn_ref[:, hs])
            o_ref[:, hs] = _diff_finish(acc, l, lam, gain_ref[h:h + 1, :], lam_init, ts).astype(BF16)


def _finish_kernel(x_ref, a_ref, gm_ref, wo_ref, g1_ref, b1_ref, wu_ref, wd_ref, g2_ref, b2_ref, y_ref,
                   *, alpha):
    mixed = jnp.concatenate([a_ref[...], gm_ref[...]], axis=1)
    r = jnp.dot(mixed, wo_ref[...], preferred_element_type=F32)
    x1 = _layer_norm(alpha * x_ref[...] + r, g1_ref[...], b1_ref[...])
    x1b = x1.astype(BF16)
    d_ff = wu_ref.shape[1]
    f = jnp.zeros(x1.shape, F32)
    for c in range(d_ff // FFN_CHUNK):
        cs = slice(c * FFN_CHUNK, (c + 1) * FFN_CHUNK)
        hc = jnp.dot(x1b, wu_ref[:, cs], preferred_element_type=F32)
        hc = jnp.square(jnp.maximum(hc, 0.0)).astype(BF16)
        f = f + jnp.dot(hc, wd_ref[cs, :], preferred_element_type=F32)
    y_ref[...] = _layer_norm(alpha * x1 + f, g2_ref[...], b2_ref[...])


def _const_spec(shape):
    nd = len(shape)
    return pl.BlockSpec(shape, lambda *_: (0,) * nd, pipeline_mode=pl.Buffered(1))


def _params(*sem):
    return pltpu.CompilerParams(dimension_semantics=sem, vmem_limit_bytes=VMEM_LIMIT)


def _project(x2d, w_in, lng, lnb, w_spatial, bsp, *, chunk, want_gn):
    n, d = x2d.shape
    tm = min(ROW_TILE, n)
    row = lambda width: pl.BlockSpec((tm, width), lambda i: (i, 0))
    out_shape = [jax.ShapeDtypeStruct((n, D_ATT), BF16),
                 jax.ShapeDtypeStruct((n, D_ATT), F32),
                 jax.ShapeDtypeStruct((n, D_ATT), F32),
                 jax.ShapeDtypeStruct((n, D_ATT), BF16),
                 jax.ShapeDtypeStruct((n, D_ATT), BF16),
                 jax.ShapeDtypeStruct((n, D_GMLP), BF16)]
    if want_gn:
        out_shape.append(jax.ShapeDtypeStruct((n, D_GMLP), F32))
    return pl.pallas_call(
        functools.partial(_proj_kernel, chunk=chunk),
        grid=(n // tm,),
        in_specs=[row(d), _const_spec(w_in.shape), _const_spec(lng.shape), _const_spec(lnb.shape),
                  _const_spec(w_spatial.shape), _const_spec(bsp.shape)],
        out_specs=[row(s.shape[1]) for s in out_shape],
        out_shape=out_shape,
        compiler_params=_params("parallel"),
        name="proj_chunk%d" % chunk,
    )(x2d, w_in, lng, lnb, w_spatial, bsp)


def _finish(x2d, attn, gm, w_out, g1, b1, w_up, w_down, g2, b2, *, alpha):
    n, d = x2d.shape
    tm = min(ROW_TILE, n)
    row = lambda width: pl.BlockSpec((tm, width), lambda i: (i, 0))
    return pl.pallas_call(
        functools.partial(_finish_kernel, alpha=alpha),
        grid=(n // tm,),
        in_specs=[row(d), row(D_ATT), row(D_GMLP), _const_spec(w_out.shape), _const_spec(g1.shape),
                  _const_spec(b1.shape), _const_spec(w_up.shape), _const_spec(w_down.shape),
                  _const_spec(g2.shape), _const_spec(b2.shape)],
        out_specs=row(d),
        out_shape=jax.ShapeDtypeStruct((n, d), F32),
        compiler_params=_params("parallel"),
        name="finish_rows%d" % n,
    )(x2d, attn, gm, w_out, g1, b1, w_up, w_down, g2, b2)


def kernel(x_prompt, x_sample, cache_k, cache_v, rel_bias_table, w_in, lambda_q1, lambda_k1, lambda_q2,
           lambda_k2, subln_gain, gmlp_ln_gain, gmlp_ln_bias, w_spatial, b_spatial, w_out, ln1_gain,
           ln1_bias, w_ffn_up, w_ffn_down, ln2_gain, ln2_bias):
    depth = w_in.shape[0]
    assert depth == 1, "single-layer step"
    b_p, t_p, d_model = x_prompt.shape
    b_s, t_s, _ = x_sample.shape
    past = cache_k.shape[2]
    alpha = (2.0 * depth) ** 0.25
    lam_init = _lambda_init(0)
    t = ATT_TILE
    tc = CACHE_TILE
    assert t_p % t == 0 and past % tc == 0 and t % CHUNK == 0 and t_s == CHUNK and past % GMLP_CHUNK == 0

    bucket_table = _bucket_of_distance(past + t_s)
    n_prompt_bias = _num_varying_tiles(t, bucket_table) + 1
    n_sample_var = _num_varying_tiles(tc, bucket_table) - 1
    n_sample_bias = n_sample_var + 1
    assert past // tc > n_sample_var

    bias_p, bias_s, bias_n = pl.pallas_call(
        functools.partial(_bias_kernel, n_prompt=n_prompt_bias, t=t, n_sample=n_sample_bias, tc=tc,
                          past=past, ts=t_s, bucket_table=bucket_table),
        in_specs=[pl.BlockSpec(memory_space=pltpu.SMEM)],
        out_shape=[jax.ShapeDtypeStruct((N_HEADS, n_prompt_bias, t, t), F32),
                   jax.ShapeDtypeStruct((N_HEADS, n_sample_bias, t_s, tc), F32),
                   jax.ShapeDtypeStruct((N_HEADS, t_s, t_s), F32)],
        compiler_params=pltpu.CompilerParams(vmem_limit_bytes=VMEM_LIMIT),
        name="rel_bias_tiles",
    )(rel_bias_table)

    w_in_b = w_in[0].astype(BF16)
    w_out_b = w_out[0].astype(BF16)
    w_up_b = w_ffn_up[0].astype(BF16)
    w_down_b = w_ffn_down[0].astype(BF16)
    lng = gmlp_ln_gain[0].reshape(1, D_GMLP)
    lnb = gmlp_ln_bias[0].reshape(1, D_GMLP)
    bsp = jnp.repeat(b_spatial[0].T, GROUP_DIM, axis=1)
    g1, b1 = ln1_gain[0].reshape(1, d_model), ln1_bias[0].reshape(1, d_model)
    g2, b2 = ln2_gain[0].reshape(1, d_model), ln2_bias[0].reshape(1, d_model)
    lamv = jnp.stack([lambda_q1[0], lambda_k1[0], lambda_q2[0], lambda_k2[0]], axis=0)
    gain = subln_gain[0]

    xp = x_prompt.reshape(b_p * t_p, d_model)
    q_p, k_p, v_p, kb_p, vb_p, gm_p = _project(xp, w_in_b, lng, lnb, w_spatial[0], bsp,
                                               chunk=GMLP_CHUNK, want_gn=False)
    nq = t_p // t
    attn_p = pl.pallas_call(
        functools.partial(_attn_prompt_kernel, t=t, n_bias=n_prompt_bias, lam_init=lam_init),
        grid=(b_p, nq),
        in_specs=[_const_spec(lamv.shape), _const_spec(gain.shape),
                  pl.BlockSpec((t, D_ATT), lambda b, i: (b * nq + i, 0)),
                  pl.BlockSpec((t_p, D_ATT), lambda b, i: (b, 0)),
                  pl.BlockSpec((t_p, D_ATT), lambda b, i: (b, 0)),
                  _const_spec(bias_p.shape)],
        out_specs=pl.BlockSpec((t, D_ATT), lambda b, i: (b * nq + i, 0)),
        out_shape=jax.ShapeDtypeStruct((b_p * t_p, D_ATT), BF16),
        compiler_params=_params("parallel", "arbitrary"),
        name="attn_prompt",
    )(lamv, gain, q_p, kb_p, vb_p, bias_p)
    y_p = _finish(xp, attn_p, gm_p, w_out_b, g1, b1, w_up_b, w_down_b, g2, b2, alpha=alpha)

    xs = x_sample.reshape(b_s * t_s, d_model)
    q_s, k_s, v_s, kb_s, vb_s, gm_s, gn_s = _project(xs, w_in_b, lng, lnb, w_spatial[0], bsp[:t_s],
                                                     chunk=t_s, want_gn=True)
    nc = past // tc
    cache_spec = pl.BlockSpec((None, None, tc, N_HEADS, D_V), lambda b, j: (0, b, j, 0, 0))
    new_spec = pl.BlockSpec((t_s, D_ATT), lambda b, j: (b, 0))
    attn_s = pl.pallas_call(
        functools.partial(_attn_sample_kernel, n_varying=n_sample_var, lam_init=lam_init),
        grid=(b_s, nc),
        in_specs=[_const_spec(lamv.shape), _const_spec(gain.shape), new_spec, new_spec, new_spec,
                  cache_spec, cache_spec, _const_spec(bias_s.shape), _const_spec(bias_n.shape)],
        out_specs=new_spec,
        out_shape=jax.ShapeDtypeStruct((b_s * t_s, D_ATT), BF16),
        scratch_shapes=[pltpu.VMEM((N_HEADS, 2 * t_s, 1), F32), pltpu.VMEM((N_HEADS, 2 * t_s, 1), F32),
                        pltpu.VMEM((N_HEADS, 2 * t_s, D_V), F32)],
        compiler_params=_params("parallel", "arbitrary"),
        name="attn_sample",
    )(lamv, gain, q_s, kb_s, vb_s, cache_k, cache_v, bias_s, bias_n)
    y_s = _finish(xs, attn_s, gm_s, w_out_b, g1, b1, w_up_b, w_down_b, g2, b2, alpha=alpha)

    head_shape_p = (depth, b_p, t_p, N_HEADS, D_V)
    head_shape_s = (depth, b_s, t_s, N_HEADS, D_V)
    return (y_p.reshape(b_p, t_p, d_model), y_s.reshape(b_s, t_s, d_model),
            k_p.reshape(head_shape_p), v_p.reshape(head_shape_p),
            k_s.reshape(head_shape_s), v_s.reshape(head_shape_s),
            gn_s.reshape(depth, b_s, t_s, N_GROUPS, GROUP_DIM))
```

```python
import functools
import math

import numpy as np
import jax
import jax.numpy as jnp
from jax import lax
from jax.experimental import pallas as pl
from jax.experimental.pallas import tpu as pltpu

F32 = jnp.float32
BF16 = jnp.bfloat16

CHUNK = 64
N_HEADS = 4
D_QK = 64
D_V = 2 * D_QK
D_ATT = N_HEADS * D_V
N_GROUPS = 4
GROUP_DIM = 128
D_GMLP = N_GROUPS * GROUP_DIM
GMLP_CHUNK = 128
NUM_BUCKETS = 32
MAX_DISTANCE = 1024
LN_EPS = 1e-5
NEG_INF = -1e30
QK_SCALE = D_QK ** -0.5

ROW_TILE = 512
ATT_TILE = 256
CACHE_TILE = 1024
FFN_CHUNK = 1024
VMEM_LIMIT = 56 * 1024 * 1024

_NT = (((1,), (1,)), ((), ()))


def _lambda_init(layer):
    return 0.8 - 0.6 * math.exp(-0.3 * layer)


def _bucket_of_distance(max_n):
    half = NUM_BUCKETS // 2
    max_exact = half // 2
    n = np.arange(max_n + 1)

    def table(dtype):
        nf = np.maximum(n, 1).astype(dtype)
        scaled = np.log(nf / dtype(max_exact)) / dtype(math.log(MAX_DISTANCE / max_exact)) * dtype(half - max_exact)
        large = np.minimum(max_exact + scaled.astype(np.int64), half - 1)
        return np.where(n < max_exact, n, large)

    t64 = table(np.float64)
    assert np.array_equal(t64, table(np.float32)), "bucket boundaries are precision sensitive"
    return t64


def _rel_bucket_static(rel, bucket_table):
    return (NUM_BUCKETS // 2 if rel > 0 else 0) + int(bucket_table[abs(rel)])


def _bias_tile(tab_ref, head, d0, nrows, ncols, bucket_table, *, keys_on_rows=False):
    row = lax.broadcasted_iota(jnp.int32, (nrows, ncols), 0)
    col = lax.broadcasted_iota(jnp.int32, (nrows, ncols), 1)
    if keys_on_rows:
        d = (row - col) + d0
        dmin, dmax = d0 - (ncols - 1), d0 + nrows - 1
    else:
        d = (col - row) + d0
        dmin, dmax = d0 - (nrows - 1), d0 + ncols - 1
    prev = _rel_bucket_static(dmin, bucket_table)
    val = jnp.full((nrows, ncols), tab_ref[prev, head], F32)
    for dd in range(dmin + 1, dmax + 1):
        b = _rel_bucket_static(dd, bucket_table)
        if b != prev:
            val = jnp.where(d >= dd, tab_ref[b, head], val)
            prev = b
    return val


def _bias_kernel(tab_ref, bp_ref, bs_ref, bn_ref, *, n_prompt, t, n_sample, tc, past, ts, bucket_table):
    key = lax.broadcasted_iota(jnp.int32, (t, t), 0)
    qry = lax.broadcasted_iota(jnp.int32, (t, t), 1)
    visible = (key // CHUNK) <= (qry // CHUNK)
    n_cache_tiles = past // tc
    for h in range(N_HEADS):
        for dl in range(n_prompt):
            tile = _bias_tile(tab_ref, h, -dl * t, t, t, bucket_table, keys_on_rows=True)
            if dl == 0:
                tile = jnp.where(visible, tile, NEG_INF)
            bp_ref[h, dl] = tile
        for i in range(n_sample):
            j = n_cache_tiles - n_sample + i
            bs_ref[h, i] = _bias_tile(tab_ref, h, j * tc - past, ts, tc, bucket_table)
        bn_ref[h] = _bias_tile(tab_ref, h, 0, ts, ts, bucket_table)


def _far_distance(bucket_table):
    n = len(bucket_table) - 1
    while n > 0 and bucket_table[n - 1] == bucket_table[-1]:
        n -= 1
    return n


def _num_varying_tiles(tile, bucket_table):
    far = _far_distance(bucket_table)
    n = 1
    while (n - 1) * tile + 1 < far:
        n += 1
    return n


def _gelu_tanh(x):
    c = math.sqrt(2.0 / math.pi)
    return 0.5 * x * (1.0 + jnp.tanh(c * (x + 0.044715 * (x * x * x))))


def _layer_norm(x, g, b):
    mu = jnp.mean(x, axis=-1, keepdims=True)
    xc = x - mu
    var = jnp.mean(xc * xc, axis=-1, keepdims=True)
    return xc * lax.rsqrt(var + LN_EPS) * g + b


def _store_heads(ref, x, n):
    rows = x.shape[0]
    for h in range(n):
        ref[pl.ds(h, rows, stride=n), :] = x[:, h * 128:(h + 1) * 128]


def _proj_kernel(x_ref, w_ref, lng_ref, lnb_ref, ws_ref, bsp_ref, *out_refs, chunk, att_tile, sample):
    if sample:
        q_ref, kf_ref, vf_ref, kb_ref, vb_ref, gm_ref, gn_ref = out_refs
    else:
        q_ref, kf_ref, vf_ref, kb_ref, vt_ref, gm_ref = out_refs
    tm = x_ref.shape[0]
    xb = x_ref[...].astype(BF16)

    def seg(lo, hi):
        return jnp.dot(xb, w_ref[:, lo:hi], preferred_element_type=F32)

    q_ref[...] = (seg(0, D_ATT) * QK_SCALE).astype(BF16)
    k = seg(D_ATT, 2 * D_ATT)
    _store_heads(kf_ref, k, N_HEADS)
    kb_ref[...] = k.astype(BF16)
    v = seg(2 * D_ATT, 3 * D_ATT)
    _store_heads(vf_ref, v, N_HEADS)
    if sample:
        vb_ref[...] = v.astype(BF16)
    else:
        for j in range(tm // att_tile):
            vt_ref[j] = v[j * att_tile:(j + 1) * att_tile, :].T.astype(BF16)
    u = _gelu_tanh(seg(3 * D_ATT, 3 * D_ATT + D_GMLP))
    g = _gelu_tanh(seg(3 * D_ATT + D_GMLP, 3 * D_ATT + 2 * D_GMLP))

    lng = lng_ref[...]
    lnb = lnb_ref[...]
    gn = jnp.concatenate(
        [_layer_norm(g[:, i * GROUP_DIM:(i + 1) * GROUP_DIM],
                     lng[:, i * GROUP_DIM:(i + 1) * GROUP_DIM],
                     lnb[:, i * GROUP_DIM:(i + 1) * GROUP_DIM]) for i in range(N_GROUPS)], axis=1)
    if sample:
        _store_heads(gn_ref, gn, N_GROUPS)
    gnb = gn.astype(BF16)

    ri = lax.broadcasted_iota(jnp.int32, (chunk, chunk), 0)
    ci = lax.broadcasted_iota(jnp.int32, (chunk, chunk), 1)
    keep = (ri // CHUNK) >= (ci // CHUNK)
    ws = [jnp.where(keep, ws_ref[i, :chunk, :chunk], 0.0).astype(BF16) for i in range(N_GROUPS)]
    bsp = bsp_ref[...]
    for c in range(tm // chunk):
        r0 = c * chunk
        sp = jnp.concatenate(
            [jnp.dot(ws[i], gnb[r0:r0 + chunk, i * GROUP_DIM:(i + 1) * GROUP_DIM],
                     preferred_element_type=F32) for i in range(N_GROUPS)], axis=1)
        gm_ref[r0:r0 + chunk, :] = (u[r0:r0 + chunk, :] * (sp + bsp)).astype(BF16)


def _diff_lambda(lamv_ref, lam_init):
    lv = lamv_ref[...]
    s1 = jnp.sum(lv[0:1, :] * lv[1:2, :], axis=-1, keepdims=True)
    s2 = jnp.sum(lv[2:3, :] * lv[3:4, :], axis=-1, keepdims=True)
    return jnp.exp(s1) - jnp.exp(s2) + lam_init


def _stack_halves(qh):
    lane = lax.broadcasted_iota(jnp.int32, qh.shape, 1)
    qf = qh.astype(F32)
    first = jnp.where(lane < D_QK, qf, 0.0)
    second = jnp.where(lane >= D_QK, qf, 0.0)
    return jnp.concatenate([first, second], axis=0).astype(BF16)


def _attn_prompt_kernel(lamv_ref, gaint_ref, q_ref, k_ref, vt_ref, bias_ref, o_ref,
                        qz_ref, m_ref, l_ref, acc_ref, s_ref, *, t, n_bias, lam_init):
    qi = pl.program_id(1)
    for h in range(N_HEADS):
        qz_ref[h] = _stack_halves(q_ref[:, h * D_V:(h + 1) * D_V])
    m_ref[...] = jnp.full(m_ref.shape, -jnp.inf, F32)
    l_ref[...] = jnp.zeros(l_ref.shape, F32)
    acc_ref[...] = jnp.zeros(acc_ref.shape, F32)

    def scores(kj, h):
        off = pl.multiple_of(kj * t, t)
        return lax.dot_general(k_ref[pl.ds(off, t), h * D_V:(h + 1) * D_V], qz_ref[h], _NT,
                               preferred_element_type=F32)

    s_ref[...] = scores(0, 0)

    def body(kj, carry):
        bi = jnp.minimum(qi - kj, n_bias - 1)
        s = s_ref[...]
        for h in range(N_HEADS):
            hs = slice(h * D_V, (h + 1) * D_V)
            if h + 1 < N_HEADS:
                s_next = scores(kj, h + 1)
            else:
                s_next = scores(jnp.minimum(kj + 1, qi), 0)
            bias = bias_ref[h, bi]
            s = s + jnp.concatenate([bias, bias], axis=1)
            m_prev = m_ref[h]
            m_new = jnp.maximum(m_prev, jnp.max(s, axis=0, keepdims=True))
            alpha = jnp.exp(m_prev - m_new)
            p = jnp.exp(s - m_new)
            l_ref[h] = alpha * l_ref[h] + jnp.sum(p, axis=0, keepdims=True)
            acc_ref[h] = alpha * acc_ref[h] + jnp.dot(vt_ref[kj, hs, :], p.astype(BF16),
                                                      preferred_element_type=F32)
            m_ref[h] = m_new
            s = s_next
        s_ref[...] = s
        return carry

    lax.fori_loop(0, qi + 1, body, 0)

    lam = _diff_lambda(lamv_ref, lam_init)
    for h in range(N_HEADS):
        o = acc_ref[h] / l_ref[h]
        od = o[:, :t] - lam * o[:, t:]
        ms = jnp.mean(od * od, axis=0, keepdims=True)
        y = od * lax.rsqrt(ms + LN_EPS) * gaint_ref[:, h:h + 1] * (1.0 - lam_init)
        o_ref[:, h * D_V:(h + 1) * D_V] = y.T.astype(BF16)


def _softmax_step(s, m, l, acc, vt):
    m_new = jnp.maximum(m, jnp.max(s, axis=-1, keepdims=True))
    alpha = jnp.exp(m - m_new)
    p = jnp.exp(s - m_new)
    l_new = alpha * l + jnp.sum(p, axis=-1, keepdims=True)
    acc_new = alpha * acc + jnp.dot(p.astype(BF16), vt, preferred_element_type=F32)
    return m_new, l_new, acc_new


def _diff_finish(acc, l, lam, gain, lam_init, t):
    o = acc / l
    od = o[:t] - lam * o[t:]
    ms = jnp.mean(od * od, axis=-1, keepdims=True)
    return od * lax.rsqrt(ms + LN_EPS) * gain * (1.0 - lam_init)


def _attn_sample_kernel(lamv_ref, gain_ref, q_ref, kn_ref, vn_ref, kc_ref, vc_ref, bias_ref, biasn_ref,
                        o_ref, m_ref, l_ref, acc_ref, *, n_varying, lam_init):
    j = pl.program_id(1)
    nj = pl.num_programs(1)
    ts = q_ref.shape[0]
    tc = kc_ref.shape[0] // N_HEADS

    @pl.when(j == 0)
    def _():
        m_ref[...] = jnp.full(m_ref.shape, -jnp.inf, F32)
        l_ref[...] = jnp.zeros(l_ref.shape, F32)
        acc_ref[...] = jnp.zeros(acc_ref.shape, F32)

    bi = jnp.maximum(j - (nj - n_varying - 1), 0)
    for h in range(N_HEADS):
        hs = slice(h * D_V, (h + 1) * D_V)
        qz = _stack_halves(q_ref[:, hs])
        kh = kc_ref[pl.ds(h, tc, stride=N_HEADS), :].astype(BF16)
        vh = vc_ref[pl.ds(h, tc, stride=N_HEADS), :].astype(BF16)
        s = lax.dot_general(qz, kh, _NT, preferred_element_type=F32)
        bias = bias_ref[h, bi]
        s = s + jnp.concatenate([bias, bias], axis=0)
        m, l, acc = _softmax_step(s, m_ref[h], l_ref[h], acc_ref[h], vh)
        m_ref[h] = m
        l_ref[h] = l
        acc_ref[h] = acc

    @pl.when(j == nj - 1)
    def _():
        lam = _diff_lambda(lamv_ref, lam_init)
        for h in range(N_HEADS):
            hs = slice(h * D_V, (h + 1) * D_V)
            qz = _stack_halves(q_ref[:, hs])
            s = lax.dot_general(qz, kn_ref[:, hs], _NT, preferred_element_type=F32)
            bias = biasn_ref[h]
            s = s + jnp.concatenate([bias, bias], axis=0)
            _, l, acc = _softmax_step(s, m_ref[h], l_ref[h], acc_ref[h], vn_ref[:, hs])
            o_ref[:, hs] = _diff_finish(acc, l, lam, gain_ref[h:h + 1, :], lam_init, ts).astype(BF16)


def _finish_kernel(x_ref, a_ref, gm_ref, wo_ref, g1_ref, b1_ref, wu_ref, wd_ref, g2_ref, b2_ref, y_ref,
                   *, alpha):
    mixed = jnp.concatenate([a_ref[...], gm_ref[...]], axis=1)
    r = jnp.dot(mixed, wo_ref[...], preferred_element_type=F32)
    x1 = _layer_norm(alpha * x_ref[...] + r, g1_ref[...], b1_ref[...])
    x1b = x1.astype(BF16)
    d_ff = wu_ref.shape[1]
    f = jnp.zeros(x1.shape, F32)
    for c in range(d_ff // FFN_CHUNK):
        cs = slice(c * FFN_CHUNK, (c + 1) * FFN_CHUNK)
        hc = jnp.dot(x1b, wu_ref[:, cs], preferred_element_type=F32)
        hc = jnp.square(jnp.maximum(hc, 0.0)).astype(BF16)
        f = f + jnp.dot(hc, wd_ref[cs, :], preferred_element_type=F32)
    y_ref[...] = _layer_norm(alpha * x1 + f, g2_ref[...], b2_ref[...])


def _const_spec(shape):
    nd = len(shape)
    return pl.BlockSpec(shape, lambda *_: (0,) * nd, pipeline_mode=pl.Buffered(1))


def _params(*sem):
    return pltpu.CompilerParams(dimension_semantics=sem, vmem_limit_bytes=VMEM_LIMIT)


def _project(x2d, w_in, lng, lnb, w_spatial, bsp, *, chunk, att_tile, sample):
    n, d = x2d.shape
    tm = min(ROW_TILE, n)
    row = lambda width: pl.BlockSpec((tm, width), lambda i: (i, 0))
    per_head = pl.BlockSpec((tm * N_HEADS, D_V), lambda i: (i, 0))
    out_shape = [jax.ShapeDtypeStruct((n, D_ATT), BF16), jax.ShapeDtypeStruct((n * N_HEADS, D_V), F32),
                 jax.ShapeDtypeStruct((n * N_HEADS, D_V), F32), jax.ShapeDtypeStruct((n, D_ATT), BF16)]
    out_specs = [row(D_ATT), per_head, per_head, row(D_ATT)]
    if sample:
        out_shape += [jax.ShapeDtypeStruct((n, D_ATT), BF16), jax.ShapeDtypeStruct((n, D_GMLP), BF16),
                      jax.ShapeDtypeStruct((n * N_GROUPS, GROUP_DIM), F32)]
        out_specs += [row(D_ATT), row(D_GMLP), pl.BlockSpec((tm * N_GROUPS, GROUP_DIM), lambda i: (i, 0))]
    else:
        out_shape += [jax.ShapeDtypeStruct((n // att_tile, D_ATT, att_tile), BF16),
                      jax.ShapeDtypeStruct((n, D_GMLP), BF16)]
        out_specs += [pl.BlockSpec((tm // att_tile, D_ATT, att_tile), lambda i: (i, 0, 0)), row(D_GMLP)]
    return pl.pallas_call(
        functools.partial(_proj_kernel, chunk=chunk, att_tile=att_tile, sample=sample),
        grid=(n // tm,),
        in_specs=[row(d), _const_spec(w_in.shape), _const_spec(lng.shape), _const_spec(lnb.shape),
                  _const_spec(w_spatial.shape), _const_spec(bsp.shape)],
        out_specs=out_specs,
        out_shape=out_shape,
        compiler_params=_params("parallel"),
        name="proj_sample" if sample else "proj_prompt",
    )(x2d, w_in, lng, lnb, w_spatial, bsp)


def _finish(x2d, attn, gm, w_out, g1, b1, w_up, w_down, g2, b2, *, alpha, name):
    n, d = x2d.shape
    tm = min(ROW_TILE, n)
    row = lambda width: pl.BlockSpec((tm, width), lambda i: (i, 0))
    return pl.pallas_call(
        functools.partial(_finish_kernel, alpha=alpha),
        grid=(n // tm,),
        in_specs=[row(d), row(D_ATT), row(D_GMLP), _const_spec(w_out.shape), _const_spec(g1.shape),
                  _const_spec(b1.shape), _const_spec(w_up.shape), _const_spec(w_down.shape),
                  _const_spec(g2.shape), _const_spec(b2.shape)],
        out_specs=row(d),
        out_shape=jax.ShapeDtypeStruct((n, d), F32),
        compiler_params=_params("parallel"),
        name=name,
    )(x2d, attn, gm, w_out, g1, b1, w_up, w_down, g2, b2)


def kernel(x_prompt, x_sample, cache_k, cache_v, rel_bias_table, w_in, lambda_q1, lambda_k1, lambda_q2,
           lambda_k2, subln_gain, gmlp_ln_gain, gmlp_ln_bias, w_spatial, b_spatial, w_out, ln1_gain,
           ln1_bias, w_ffn_up, w_ffn_down, ln2_gain, ln2_bias):
    depth = w_in.shape[0]
    assert depth == 1, "single-layer step"
    b_p, t_p, d_model = x_prompt.shape
    b_s, t_s, _ = x_sample.shape
    past = cache_k.shape[2]
    alpha = (2.0 * depth) ** 0.25
    lam_init = _lambda_init(0)
    t = ATT_TILE
    tc = CACHE_TILE
    assert t_p % t == 0 and past % tc == 0 and t % CHUNK == 0 and t_s == CHUNK and past % GMLP_CHUNK == 0
    assert ROW_TILE % t == 0 and (b_p * t_p) % ROW_TILE == 0 and (b_s * t_s) % ROW_TILE == 0

    bucket_table = _bucket_of_distance(past + t_s)
    n_prompt_bias = _num_varying_tiles(t, bucket_table) + 1
    n_sample_var = _num_varying_tiles(tc, bucket_table) - 1
    n_sample_bias = n_sample_var + 1
    assert past // tc > n_sample_var

    bias_p, bias_s, bias_n = pl.pallas_call(
        functools.partial(_bias_kernel, n_prompt=n_prompt_bias, t=t, n_sample=n_sample_bias, tc=tc,
                          past=past, ts=t_s, bucket_table=bucket_table),
        in_specs=[pl.BlockSpec(memory_space=pltpu.SMEM)],
        out_shape=[jax.ShapeDtypeStruct((N_HEADS, n_prompt_bias, t, t), F32),
                   jax.ShapeDtypeStruct((N_HEADS, n_sample_bias, t_s, tc), F32),
                   jax.ShapeDtypeStruct((N_HEADS, t_s, t_s), F32)],
        compiler_params=pltpu.CompilerParams(vmem_limit_bytes=VMEM_LIMIT),
        name="rel_bias_tiles",
    )(rel_bias_table)

    w_in_b = w_in[0].astype(BF16)
    w_out_b = w_out[0].astype(BF16)
    w_up_b = w_ffn_up[0].astype(BF16)
    w_down_b = w_ffn_down[0].astype(BF16)
    lng = gmlp_ln_gain[0].reshape(1, D_GMLP)
    lnb = gmlp_ln_bias[0].reshape(1, D_GMLP)
    bsp = jnp.repeat(b_spatial[0].T, GROUP_DIM, axis=1)
    g1, b1 = ln1_gain[0].reshape(1, d_model), ln1_bias[0].reshape(1, d_model)
    g2, b2 = ln2_gain[0].reshape(1, d_model), ln2_bias[0].reshape(1, d_model)
    lamv = jnp.stack([lambda_q1[0], lambda_k1[0], lambda_q2[0], lambda_k2[0]], axis=0)
    gain = subln_gain[0]

    xp = x_prompt.reshape(b_p * t_p, d_model)
    q_p, k_p, v_p, kb_p, vt_p, gm_p = _project(xp, w_in_b, lng, lnb, w_spatial[0], bsp,
                                               chunk=GMLP_CHUNK, att_tile=t, sample=False)
    nq = t_p // t
    gain_t = gain.T
    attn_p = pl.pallas_call(
        functools.partial(_attn_prompt_kernel, t=t, n_bias=n_prompt_bias, lam_init=lam_init),
        grid=(b_p, nq),
        in_specs=[_const_spec(lamv.shape), _const_spec(gain_t.shape),
                  pl.BlockSpec((t, D_ATT), lambda b, i: (b * nq + i, 0)),
                  pl.BlockSpec((t_p, D_ATT), lambda b, i: (b, 0)),
                  pl.BlockSpec((nq, D_ATT, t), lambda b, i: (b, 0, 0)),
                  _const_spec(bias_p.shape)],
        out_specs=pl.BlockSpec((t, D_ATT), lambda b, i: (b * nq + i, 0)),
        out_shape=jax.ShapeDtypeStruct((b_p * t_p, D_ATT), BF16),
        scratch_shapes=[pltpu.VMEM((N_HEADS, 2 * t, D_V), BF16), pltpu.VMEM((N_HEADS, 1, 2 * t), F32),
                        pltpu.VMEM((N_HEADS, 1, 2 * t), F32), pltpu.VMEM((N_HEADS, D_V, 2 * t), F32),
                        pltpu.VMEM((t, 2 * t), F32)],
        compiler_params=_params("parallel", "arbitrary"),
        name="attn_prompt",
    )(lamv, gain_t, q_p, kb_p, vt_p, bias_p)
    y_p = _finish(xp, attn_p, gm_p, w_out_b, g1, b1, w_up_b, w_down_b, g2, b2, alpha=alpha,
                  name="finish_prompt")

    xs = x_sample.reshape(b_s * t_s, d_model)
    q_s, k_s, v_s, kb_s, vb_s, gm_s, gn_s = _project(xs, w_in_b, lng, lnb, w_spatial[0], bsp[:t_s],
                                                     chunk=t_s, att_tile=t, sample=True)
    nc = past // tc
    ck = cache_k.reshape(b_s, past * N_HEADS, D_V)
    cv = cache_v.reshape(b_s, past * N_HEADS, D_V)
    cache_spec = pl.BlockSpec((None, tc * N_HEADS, D_V), lambda b, j: (b, j, 0))
    new_spec = pl.BlockSpec((t_s, D_ATT), lambda b, j: (b, 0))
    attn_s = pl.pallas_call(
        functools.partial(_attn_sample_kernel, n_varying=n_sample_var, lam_init=lam_init),
        grid=(b_s, nc),
        in_specs=[_const_spec(lamv.shape), _const_spec(gain.shape), new_spec, new_spec, new_spec,
                  cache_spec, cache_spec, _const_spec(bias_s.shape), _const_spec(bias_n.shape)],
        out_specs=new_spec,
        out_shape=jax.ShapeDtypeStruct((b_s * t_s, D_ATT), BF16),
        scratch_shapes=[pltpu.VMEM((N_HEADS, 2 * t_s, 1), F32), pltpu.VMEM((N_HEADS, 2 * t_s, 1), F32),
                        pltpu.VMEM((N_HEADS, 2 * t_s, D_V), F32)],
        compiler_params=_params("parallel", "arbitrary"),
        name="attn_sample",
    )(lamv, gain, q_s, kb_s, vb_s, ck, cv, bias_s, bias_n)
    y_s = _finish(xs, attn_s, gm_s, w_out_b, g1, b1, w_up_b, w_down_b, g2, b2, alpha=alpha,
                  name="finish_sample")

    head_shape_p = (depth, b_p, t_p, N_HEADS, D_V)
    head_shape_s = (depth, b_s, t_s, N_HEADS, D_V)
    return (y_p.reshape(b_p, t_p, d_model), y_s.reshape(b_s, t_s, d_model),
            k_p.reshape(head_shape_p), v_p.reshape(head_shape_p),
            k_s.reshape(head_shape_s), v_s.reshape(head_shape_s),
            gn_s.reshape(depth, b_s, t_s, N_GROUPS, GROUP_DIM))
```

```python
import functools
import math

import numpy as np
import jax
import jax.numpy as jnp
from jax import lax
from jax.experimental import pallas as pl
from jax.experimental.pallas import tpu as pltpu

F32 = jnp.float32
BF16 = jnp.bfloat16

CHUNK = 64
N_HEADS = 4
D_QK = 64
D_V = 2 * D_QK
D_ATT = N_HEADS * D_V
N_GROUPS = 4
GROUP_DIM = 128
D_GMLP = N_GROUPS * GROUP_DIM
GMLP_CHUNK = 128
NUM_BUCKETS = 32
MAX_DISTANCE = 1024
LN_EPS = 1e-5
NEG_INF = -1e30
QK_SCALE = D_QK ** -0.5
LOG2E = math.log2(math.e)
ONES_ROWS = 16
VT_ROWS = D_V + ONES_ROWS

ROW_TILE = 512
ATT_TILE = 256
CACHE_TILE = 2048
FFN_CHUNK = 1024
VMEM_LIMIT = 56 * 1024 * 1024

_NT = (((1,), (1,)), ((), ()))


def _lambda_init(layer):
    return 0.8 - 0.6 * math.exp(-0.3 * layer)


def _bucket_of_distance(max_n):
    half = NUM_BUCKETS // 2
    max_exact = half // 2
    n = np.arange(max_n + 1)

    def table(dtype):
        nf = np.maximum(n, 1).astype(dtype)
        scaled = np.log(nf / dtype(max_exact)) / dtype(math.log(MAX_DISTANCE / max_exact)) * dtype(half - max_exact)
        large = np.minimum(max_exact + scaled.astype(np.int64), half - 1)
        return np.where(n < max_exact, n, large)

    t64 = table(np.float64)
    assert np.array_equal(t64, table(np.float32)), "bucket boundaries are precision sensitive"
    return t64


def _rel_bucket_static(rel, bucket_table):
    return (NUM_BUCKETS // 2 if rel > 0 else 0) + int(bucket_table[abs(rel)])


def _bias_tile(tab_ref, head, d0, nrows, ncols, bucket_table, *, keys_on_rows=False):
    row = lax.broadcasted_iota(jnp.int32, (nrows, ncols), 0)
    col = lax.broadcasted_iota(jnp.int32, (nrows, ncols), 1)
    if keys_on_rows:
        d = (row - col) + d0
        dmin, dmax = d0 - (ncols - 1), d0 + nrows - 1
    else:
        d = (col - row) + d0
        dmin, dmax = d0 - (nrows - 1), d0 + ncols - 1
    prev = _rel_bucket_static(dmin, bucket_table)
    val = jnp.full((nrows, ncols), tab_ref[prev, head] * LOG2E, F32)
    for dd in range(dmin + 1, dmax + 1):
        b = _rel_bucket_static(dd, bucket_table)
        if b != prev:
            val = jnp.where(d >= dd, tab_ref[b, head] * LOG2E, val)
            prev = b
    return val


def _bias_kernel(tab_ref, bp_ref, bs_ref, bn_ref, *, n_prompt, t, n_sample, tc, past, ts, bucket_table):
    key = lax.broadcasted_iota(jnp.int32, (t, t), 0)
    qry = lax.broadcasted_iota(jnp.int32, (t, t), 1)
    visible = (key // CHUNK) <= (qry // CHUNK)
    n_cache_tiles = past // tc
    for h in range(N_HEADS):
        for dl in range(n_prompt):
            tile = _bias_tile(tab_ref, h, -dl * t, t, t, bucket_table, keys_on_rows=True)
            if dl == 0:
                tile = jnp.where(visible, tile, NEG_INF)
            bp_ref[h, dl] = tile
        for i in range(n_sample):
            j = n_cache_tiles - n_sample + i
            bs_ref[h, i] = _bias_tile(tab_ref, h, j * tc - past, ts, tc, bucket_table)
        bn_ref[h] = _bias_tile(tab_ref, h, 0, ts, ts, bucket_table)


def _far_distance(bucket_table):
    n = len(bucket_table) - 1
    while n > 0 and bucket_table[n - 1] == bucket_table[-1]:
        n -= 1
    return n


def _num_varying_tiles(tile, bucket_table):
    far = _far_distance(bucket_table)
    n = 1
    while (n - 1) * tile + 1 < far:
        n += 1
    return n


def _gelu_tanh(x):
    c = math.sqrt(2.0 / math.pi)
    return 0.5 * x * (1.0 + jnp.tanh(c * (x + 0.044715 * (x * x * x))))


def _layer_norm(x, g, b):
    mu = jnp.mean(x, axis=-1, keepdims=True)
    xc = x - mu
    var = jnp.mean(xc * xc, axis=-1, keepdims=True)
    return xc * lax.rsqrt(var + LN_EPS) * g + b


def _store_heads(ref, x, n):
    rows = x.shape[0]
    for h in range(n):
        ref[pl.ds(h, rows, stride=n), :] = x[:, h * 128:(h + 1) * 128]


def _proj_kernel(x_ref, w_ref, lng_ref, lnb_ref, ws_ref, bsp_ref, *out_refs, chunk, att_tile, sample):
    if sample:
        q_ref, kf_ref, vf_ref, kb_ref, vb_ref, gm_ref, gn_ref = out_refs
    else:
        q_ref, kf_ref, vf_ref, kb_ref, vt_ref, gm_ref = out_refs
    tm = x_ref.shape[0]
    xb = x_ref[...].astype(BF16)

    def seg(lo, hi):
        return jnp.dot(xb, w_ref[:, lo:hi], preferred_element_type=F32)

    q_ref[...] = (seg(0, D_ATT) * (QK_SCALE * LOG2E)).astype(BF16)
    k = seg(D_ATT, 2 * D_ATT)
    _store_heads(kf_ref, k, N_HEADS)
    kb_ref[...] = k.astype(BF16)
    v = seg(2 * D_ATT, 3 * D_ATT)
    _store_heads(vf_ref, v, N_HEADS)
    if sample:
        vb_ref[...] = v.astype(BF16)
    else:
        ones = jnp.ones((ONES_ROWS, att_tile), F32)
        for j in range(tm // att_tile):
            vt = v[j * att_tile:(j + 1) * att_tile, :].T
            vt_ref[j] = jnp.concatenate(
                [part for h in range(N_HEADS) for part in (vt[h * D_V:(h + 1) * D_V], ones)],
                axis=0).astype(BF16)
    u = _gelu_tanh(seg(3 * D_ATT, 3 * D_ATT + D_GMLP))
    g = _gelu_tanh(seg(3 * D_ATT + D_GMLP, 3 * D_ATT + 2 * D_GMLP))

    lng = lng_ref[...]
    lnb = lnb_ref[...]
    gn = jnp.concatenate(
        [_layer_norm(g[:, i * GROUP_DIM:(i + 1) * GROUP_DIM],
                     lng[:, i * GROUP_DIM:(i + 1) * GROUP_DIM],
                     lnb[:, i * GROUP_DIM:(i + 1) * GROUP_DIM]) for i in range(N_GROUPS)], axis=1)
    if sample:
        _store_heads(gn_ref, gn, N_GROUPS)
    gnb = gn.astype(BF16)

    ri = lax.broadcasted_iota(jnp.int32, (chunk, chunk), 0)
    ci = lax.broadcasted_iota(jnp.int32, (chunk, chunk), 1)
    keep = (ri // CHUNK) >= (ci // CHUNK)
    ws = [jnp.where(keep, ws_ref[i, :chunk, :chunk], 0.0).astype(BF16) for i in range(N_GROUPS)]
    bsp = bsp_ref[...]
    for c in range(tm // chunk):
        r0 = c * chunk
        sp = jnp.concatenate(
            [jnp.dot(ws[i], gnb[r0:r0 + chunk, i * GROUP_DIM:(i + 1) * GROUP_DIM],
                     preferred_element_type=F32) for i in range(N_GROUPS)], axis=1)
        gm_ref[r0:r0 + chunk, :] = (u[r0:r0 + chunk, :] * (sp + bsp)).astype(BF16)


def _diff_lambda(lamv_ref, lam_init):
    lv = lamv_ref[...]
    s1 = jnp.sum(lv[0:1, :] * lv[1:2, :], axis=-1, keepdims=True)
    s2 = jnp.sum(lv[2:3, :] * lv[3:4, :], axis=-1, keepdims=True)
    return jnp.exp(s1) - jnp.exp(s2) + lam_init


def _stack_halves(qh):
    lane = lax.broadcasted_iota(jnp.int32, qh.shape, 1)
    qf = qh.astype(F32)
    first = jnp.where(lane < D_QK, qf, 0.0)
    second = jnp.where(lane >= D_QK, qf, 0.0)
    return jnp.concatenate([first, second], axis=0).astype(BF16)


def _attn_prompt_kernel(lamv_ref, gaint_ref, q_ref, k_ref, vt_ref, bias_ref, o_ref,
                        qz_ref, m_ref, acc_ref, s_ref, *, t, n_bias, lam_init):
    qi = pl.program_id(1)
    for h in range(N_HEADS):
        qz_ref[h] = _stack_halves(q_ref[:, h * D_V:(h + 1) * D_V])
    m_ref[...] = jnp.full(m_ref.shape, -jnp.inf, F32)
    acc_ref[...] = jnp.zeros(acc_ref.shape, F32)

    n_tiles = qi + 1

    def pass1(kj, width):
        off = pl.multiple_of(kj * t, t)
        for h in range(N_HEADS):
            s = lax.dot_general(k_ref[pl.ds(off, width * t), h * D_V:(h + 1) * D_V], qz_ref[h], _NT,
                                preferred_element_type=F32)
            tiles = [bias_ref[h, jnp.minimum(qi - (kj + i), n_bias - 1)] for i in range(width)]
            s = s + jnp.concatenate([jnp.concatenate([b, b], axis=1) for b in tiles], axis=0)
            s_ref[h, pl.ds(off, width * t), :] = s
            m_ref[h] = jnp.maximum(m_ref[h], jnp.max(s, axis=0, keepdims=True))

    def pass2(kj, width):
        off = pl.multiple_of(kj * t, t)
        for h in range(N_HEADS):
            p = jnp.exp2(s_ref[h, pl.ds(off, width * t), :] - m_ref[h])
            vt = jnp.concatenate([vt_ref[kj + i, h * VT_ROWS:(h + 1) * VT_ROWS, :] for i in range(width)],
                                 axis=1)
            acc_ref[h] = acc_ref[h] + jnp.dot(vt, p.astype(BF16), preferred_element_type=F32)

    for one_pass in (pass1, pass2):
        def pair(i, carry, one_pass=one_pass):
            one_pass(2 * i, 2)
            return carry

        lax.fori_loop(0, n_tiles // 2, pair, 0)

        @pl.when(n_tiles % 2 == 1)
        def _(one_pass=one_pass):
            one_pass(n_tiles - 1, 1)


    lam = _diff_lambda(lamv_ref, lam_init)
    for h in range(N_HEADS):
        acc = acc_ref[h]
        o = acc[:D_V] / acc[D_V:D_V + 1]
        od = o[:, :t] - lam * o[:, t:]
        ms = jnp.mean(od * od, axis=0, keepdims=True)
        y = od * lax.rsqrt(ms + LN_EPS) * gaint_ref[:, h:h + 1] * (1.0 - lam_init)
        o_ref[:, h * D_V:(h + 1) * D_V] = y.T.astype(BF16)


def _attn_sample_kernel(lamv_ref, gain_ref, q_ref, kn_ref, vn_ref, kc_ref, vc_ref, bias_ref, biasn_ref,
                        o_ref, qz_ref, m_ref, acc_ref, s_ref, sn_ref, *, n_varying, lam_init):
    j = pl.program_id(1)
    nk = pl.num_programs(1) // 2
    ts = q_ref.shape[0]
    tc = kc_ref.shape[0] // N_HEADS

    @pl.when(j == 0)
    def _():
        for h in range(N_HEADS):
            qz_ref[h] = _stack_halves(q_ref[:, h * D_V:(h + 1) * D_V])
        m_ref[...] = jnp.full(m_ref.shape, -jnp.inf, F32)
        acc_ref[...] = jnp.zeros(acc_ref.shape, F32)

    @pl.when(j < nk)
    def _():
        bi = jnp.maximum(j - (nk - n_varying - 1), 0)
        for h in range(N_HEADS):
            kh = kc_ref[pl.ds(h, tc, stride=N_HEADS), :].astype(BF16)
            s = lax.dot_general(qz_ref[h], kh, _NT, preferred_element_type=F32)
            bias = bias_ref[h, bi]
            s = s + jnp.concatenate([bias, bias], axis=0)
            s_ref[h, j] = s
            m_ref[h] = jnp.maximum(m_ref[h], jnp.max(s, axis=-1, keepdims=True))

    @pl.when(j == nk - 1)
    def _():
        for h in range(N_HEADS):
            s = lax.dot_general(qz_ref[h], kn_ref[:, h * D_V:(h + 1) * D_V], _NT,
                                preferred_element_type=F32)
            bias = biasn_ref[h]
            s = s + jnp.concatenate([bias, bias], axis=0)
            sn_ref[h] = s
            m_ref[h] = jnp.maximum(m_ref[h], jnp.max(s, axis=-1, keepdims=True))

    def weighted_values(p, v):
        v1 = jnp.concatenate([v, jnp.ones(v.shape, BF16)], axis=1)
        return jnp.dot(p.astype(BF16), v1, preferred_element_type=F32)

    @pl.when(j >= nk)
    def _():
        for h in range(N_HEADS):
            p = jnp.exp2(s_ref[h, j - nk] - m_ref[h])
            vh = vc_ref[pl.ds(h, tc, stride=N_HEADS), :].astype(BF16)
            acc_ref[h] = acc_ref[h] + weighted_values(p, vh)

    @pl.when(j == 2 * nk - 1)
    def _():
        lam = _diff_lambda(lamv_ref, lam_init)
        for h in range(N_HEADS):
            hs = slice(h * D_V, (h + 1) * D_V)
            p = jnp.exp2(sn_ref[h] - m_ref[h])
            acc = acc_ref[h] + weighted_values(p, vn_ref[:, hs])
            o = acc[:, :D_V] / acc[:, D_V:D_V + 1]
            od = o[:ts] - lam * o[ts:]
            ms = jnp.mean(od * od, axis=-1, keepdims=True)
            y = od * lax.rsqrt(ms + LN_EPS) * gain_ref[h:h + 1, :] * (1.0 - lam_init)
            o_ref[:, hs] = y.astype(BF16)


def _finish_kernel(x_ref, a_ref, gm_ref, wo_ref, g1_ref, b1_ref, wu_ref, wd_ref, g2_ref, b2_ref, y_ref,
                   *, alpha):
    mixed = jnp.concatenate([a_ref[...], gm_ref[...]], axis=1)
    r = jnp.dot(mixed, wo_ref[...], preferred_element_type=F32)
    x1 = _layer_norm(alpha * x_ref[...] + r, g1_ref[...], b1_ref[...])
    x1b = x1.astype(BF16)
    d_ff = wu_ref.shape[1]
    f = jnp.zeros(x1.shape, F32)
    for c in range(d_ff // FFN_CHUNK):
        cs = slice(c * FFN_CHUNK, (c + 1) * FFN_CHUNK)
        hc = jnp.dot(x1b, wu_ref[:, cs], preferred_element_type=F32)
        hc = jnp.square(jnp.maximum(hc, 0.0)).astype(BF16)
        f = f + jnp.dot(hc, wd_ref[cs, :], preferred_element_type=F32)
    y_ref[...] = _layer_norm(alpha * x1 + f, g2_ref[...], b2_ref[...])


def _const_spec(shape):
    nd = len(shape)
    return pl.BlockSpec(shape, lambda *_: (0,) * nd, pipeline_mode=pl.Buffered(1))


def _params(*sem):
    return pltpu.CompilerParams(dimension_semantics=sem, vmem_limit_bytes=VMEM_LIMIT)


def _project(x2d, w_in, lng, lnb, w_spatial, bsp, *, chunk, att_tile, sample):
    n, d = x2d.shape
    tm = min(ROW_TILE, n)
    row = lambda width: pl.BlockSpec((tm, width), lambda i: (i, 0))
    per_head = pl.BlockSpec((tm * N_HEADS, D_V), lambda i: (i, 0))
    out_shape = [jax.ShapeDtypeStruct((n, D_ATT), BF16), jax.ShapeDtypeStruct((n * N_HEADS, D_V), F32),
                 jax.ShapeDtypeStruct((n * N_HEADS, D_V), F32), jax.ShapeDtypeStruct((n, D_ATT), BF16)]
    out_specs = [row(D_ATT), per_head, per_head, row(D_ATT)]
    if sample:
        out_shape += [jax.ShapeDtypeStruct((n, D_ATT), BF16), jax.ShapeDtypeStruct((n, D_GMLP), BF16),
                      jax.ShapeDtypeStruct((n * N_GROUPS, GROUP_DIM), F32)]
        out_specs += [row(D_ATT), row(D_GMLP), pl.BlockSpec((tm * N_GROUPS, GROUP_DIM), lambda i: (i, 0))]
    else:
        out_shape += [jax.ShapeDtypeStruct((n // att_tile, N_HEADS * VT_ROWS, att_tile), BF16),
                      jax.ShapeDtypeStruct((n, D_GMLP), BF16)]
        out_specs += [pl.BlockSpec((tm // att_tile, N_HEADS * VT_ROWS, att_tile), lambda i: (i, 0, 0)), row(D_GMLP)]
    return pl.pallas_call(
        functools.partial(_proj_kernel, chunk=chunk, att_tile=att_tile, sample=sample),
        grid=(n // tm,),
        in_specs=[row(d), _const_spec(w_in.shape), _const_spec(lng.shape), _const_spec(lnb.shape),
                  _const_spec(w_spatial.shape), _const_spec(bsp.shape)],
        out_specs=out_specs,
        out_shape=out_shape,
        compiler_params=_params("parallel"),
        name="proj_sample" if sample else "proj_prompt",
    )(x2d, w_in, lng, lnb, w_spatial, bsp)


def _finish(x2d, attn, gm, w_out, g1, b1, w_up, w_down, g2, b2, *, alpha, name):
    n, d = x2d.shape
    tm = min(ROW_TILE, n)
    row = lambda width: pl.BlockSpec((tm, width), lambda i: (i, 0))
    return pl.pallas_call(
        functools.partial(_finish_kernel, alpha=alpha),
        grid=(n // tm,),
        in_specs=[row(d), row(D_ATT), row(D_GMLP), _const_spec(w_out.shape), _const_spec(g1.shape),
                  _const_spec(b1.shape), _const_spec(w_up.shape), _const_spec(w_down.shape),
                  _const_spec(g2.shape), _const_spec(b2.shape)],
        out_specs=row(d),
        out_shape=jax.ShapeDtypeStruct((n, d), F32),
        compiler_params=_params("parallel"),
        name=name,
    )(x2d, attn, gm, w_out, g1, b1, w_up, w_down, g2, b2)


def kernel(x_prompt, x_sample, cache_k, cache_v, rel_bias_table, w_in, lambda_q1, lambda_k1, lambda_q2,
           lambda_k2, subln_gain, gmlp_ln_gain, gmlp_ln_bias, w_spatial, b_spatial, w_out, ln1_gain,
           ln1_bias, w_ffn_up, w_ffn_down, ln2_gain, ln2_bias):
    depth = w_in.shape[0]
    assert depth == 1, "single-layer step"
    b_p, t_p, d_model = x_prompt.shape
    b_s, t_s, _ = x_sample.shape
    past = cache_k.shape[2]
    alpha = (2.0 * depth) ** 0.25
    lam_init = _lambda_init(0)
    t = ATT_TILE
    tc = CACHE_TILE
    assert t_p % t == 0 and past % tc == 0 and t % CHUNK == 0 and t_s == CHUNK and past % GMLP_CHUNK == 0
    assert ROW_TILE % t == 0 and (b_p * t_p) % ROW_TILE == 0 and (b_s * t_s) % ROW_TILE == 0

    bucket_table = _bucket_of_distance(past + t_s)
    n_prompt_bias = _num_varying_tiles(t, bucket_table) + 1
    n_sample_var = _num_varying_tiles(tc, bucket_table) - 1
    n_sample_bias = n_sample_var + 1
    assert past // tc > n_sample_var

    bias_p, bias_s, bias_n = pl.pallas_call(
        functools.partial(_bias_kernel, n_prompt=n_prompt_bias, t=t, n_sample=n_sample_bias, tc=tc,
                          past=past, ts=t_s, bucket_table=bucket_table),
        in_specs=[pl.BlockSpec(memory_space=pltpu.SMEM)],
        out_shape=[jax.ShapeDtypeStruct((N_HEADS, n_prompt_bias, t, t), F32),
                   jax.ShapeDtypeStruct((N_HEADS, n_sample_bias, t_s, tc), F32),
                   jax.ShapeDtypeStruct((N_HEADS, t_s, t_s), F32)],
        compiler_params=pltpu.CompilerParams(vmem_limit_bytes=VMEM_LIMIT),
        name="rel_bias_tiles",
    )(rel_bias_table)

    w_in_b = w_in[0].astype(BF16)
    w_out_b = w_out[0].astype(BF16)
    w_up_b = w_ffn_up[0].astype(BF16)
    w_down_b = w_ffn_down[0].astype(BF16)
    lng = gmlp_ln_gain[0].reshape(1, D_GMLP)
    lnb = gmlp_ln_bias[0].reshape(1, D_GMLP)
    bsp = jnp.repeat(b_spatial[0].T, GROUP_DIM, axis=1)
    g1, b1 = ln1_gain[0].reshape(1, d_model), ln1_bias[0].reshape(1, d_model)
    g2, b2 = ln2_gain[0].reshape(1, d_model), ln2_bias[0].reshape(1, d_model)
    lamv = jnp.stack([lambda_q1[0], lambda_k1[0], lambda_q2[0], lambda_k2[0]], axis=0)
    gain = subln_gain[0]

    xp = x_prompt.reshape(b_p * t_p, d_model)
    q_p, k_p, v_p, kb_p, vt_p, gm_p = _project(xp, w_in_b, lng, lnb, w_spatial[0], bsp,
                                               chunk=GMLP_CHUNK, att_tile=t, sample=False)
    nq = t_p // t
    gain_t = gain.T
    attn_p = pl.pallas_call(
        functools.partial(_attn_prompt_kernel, t=t, n_bias=n_prompt_bias, lam_init=lam_init),
        grid=(b_p, nq),
        in_specs=[_const_spec(lamv.shape), _const_spec(gain_t.shape),
                  pl.BlockSpec((t, D_ATT), lambda b, i: (b * nq + i, 0)),
                  pl.BlockSpec((t_p, D_ATT), lambda b, i: (b, 0)),
                  pl.BlockSpec((nq, N_HEADS * VT_ROWS, t), lambda b, i: (b, 0, 0)),
                  _const_spec(bias_p.shape)],
        out_specs=pl.BlockSpec((t, D_ATT), lambda b, i: (b * nq + i, 0)),
        out_shape=jax.ShapeDtypeStruct((b_p * t_p, D_ATT), BF16),
        scratch_shapes=[pltpu.VMEM((N_HEADS, 2 * t, D_V), BF16), pltpu.VMEM((N_HEADS, 1, 2 * t), F32),
                        pltpu.VMEM((N_HEADS, VT_ROWS, 2 * t), F32),
                        pltpu.VMEM((N_HEADS, t_p, 2 * t), F32)],
        compiler_params=_params("parallel", "arbitrary"),
        name="attn_prompt",
    )(lamv, gain_t, q_p, kb_p, vt_p, bias_p)
    y_p = _finish(xp, attn_p, gm_p, w_out_b, g1, b1, w_up_b, w_down_b, g2, b2, alpha=alpha,
                  name="finish_prompt")

    xs = x_sample.reshape(b_s * t_s, d_model)
    q_s, k_s, v_s, kb_s, vb_s, gm_s, gn_s = _project(xs, w_in_b, lng, lnb, w_spatial[0], bsp[:t_s],
                                                     chunk=t_s, att_tile=t, sample=True)
    nc = past // tc
    ck = cache_k.reshape(b_s, past * N_HEADS, D_V)
    cv = cache_v.reshape(b_s, past * N_HEADS, D_V)
    key_spec = pl.BlockSpec((None, tc * N_HEADS, D_V), lambda b, j: (b, jnp.minimum(j, nc - 1), 0))
    val_spec = pl.BlockSpec((None, tc * N_HEADS, D_V), lambda b, j: (b, jnp.maximum(j - nc, 0), 0))
    new_spec = pl.BlockSpec((t_s, D_ATT), lambda b, j: (b, 0))
    attn_s = pl.pallas_call(
        functools.partial(_attn_sample_kernel, n_varying=n_sample_var, lam_init=lam_init),
        grid=(b_s, 2 * nc),
        in_specs=[_const_spec(lamv.shape), _const_spec(gain.shape), new_spec, new_spec, new_spec,
                  key_spec, val_spec, _const_spec(bias_s.shape), _const_spec(bias_n.shape)],
        out_specs=new_spec,
        out_shape=jax.ShapeDtypeStruct((b_s * t_s, D_ATT), BF16),
        scratch_shapes=[pltpu.VMEM((N_HEADS, 2 * t_s, D_V), BF16), pltpu.VMEM((N_HEADS, 2 * t_s, 1), F32),
                        pltpu.VMEM((N_HEADS, 2 * t_s, 2 * D_V), F32),
                        pltpu.VMEM((N_HEADS, nc, 2 * t_s, tc), F32),
                        pltpu.VMEM((N_HEADS, 2 * t_s, t_s), F32)],
        compiler_params=_params("parallel", "arbitrary"),
        name="attn_sample",
    )(lamv, gain, q_s, kb_s, vb_s, ck, cv, bias_s, bias_n)
    y_s = _finish(xs, attn_s, gm_s, w_out_b, g1, b1, w_up_b, w_down_b, g2, b2, alpha=alpha,
                  name="finish_sample")

    head_shape_p = (depth, b_p, t_p, N_HEADS, D_V)
    head_shape_s = (depth, b_s, t_s, N_HEADS, D_V)
    return (y_p.reshape(b_p, t_p, d_model), y_s.reshape(b_s, t_s, d_model),
            k_p.reshape(head_shape_p), v_p.reshape(head_shape_p),
            k_s.reshape(head_shape_s), v_s.reshape(head_shape_s),
            gn_s.reshape(depth, b_s, t_s, N_GROUPS, GROUP_DIM))
```

```python
import functools
import math

import numpy as np
import jax
import jax.numpy as jnp
from jax import lax
from jax.experimental import pallas as pl
from jax.experimental.pallas import tpu as pltpu

F32 = jnp.float32
BF16 = jnp.bfloat16

CHUNK = 64
N_HEADS = 4
D_QK = 64
D_V = 2 * D_QK
D_ATT = N_HEADS * D_V
N_GROUPS = 4
GROUP_DIM = 128
D_GMLP = N_GROUPS * GROUP_DIM
GMLP_CHUNK = 128
NUM_BUCKETS = 32
MAX_DISTANCE = 1024
LN_EPS = 1e-5
NEG_INF = -1e30
QK_SCALE = D_QK ** -0.5
LOG2E = math.log2(math.e)
ONES_ROWS = 16
VT_ROWS = D_V + ONES_ROWS

ROW_TILE = 512
ATT_TILE = 256
CACHE_TILE = 2048
FFN_CHUNK = 1024
VMEM_LIMIT = 56 * 1024 * 1024

_NT = (((1,), (1,)), ((), ()))


def _lambda_init(layer):
    return 0.8 - 0.6 * math.exp(-0.3 * layer)


def _bucket_of_distance(max_n):
    half = NUM_BUCKETS // 2
    max_exact = half // 2
    n = np.arange(max_n + 1)

    def table(dtype):
        nf = np.maximum(n, 1).astype(dtype)
        scaled = np.log(nf / dtype(max_exact)) / dtype(math.log(MAX_DISTANCE / max_exact)) * dtype(half - max_exact)
        large = np.minimum(max_exact + scaled.astype(np.int64), half - 1)
        return np.where(n < max_exact, n, large)

    t64 = table(np.float64)
    assert np.array_equal(t64, table(np.float32)), "bucket boundaries are precision sensitive"
    return t64


def _rel_bucket_static(rel, bucket_table):
    return (NUM_BUCKETS // 2 if rel > 0 else 0) + int(bucket_table[abs(rel)])


def _bias_tile(tab_ref, head, d0, nrows, ncols, bucket_table, *, keys_on_rows=False):
    row = lax.broadcasted_iota(jnp.int32, (nrows, ncols), 0)
    col = lax.broadcasted_iota(jnp.int32, (nrows, ncols), 1)
    if keys_on_rows:
        d = (row - col) + d0
        dmin, dmax = d0 - (ncols - 1), d0 + nrows - 1
    else:
        d = (col - row) + d0
        dmin, dmax = d0 - (nrows - 1), d0 + ncols - 1
    prev = _rel_bucket_static(dmin, bucket_table)
    val = jnp.full((nrows, ncols), tab_ref[prev, head] * LOG2E, F32)
    for dd in range(dmin + 1, dmax + 1):
        b = _rel_bucket_static(dd, bucket_table)
        if b != prev:
            val = jnp.where(d >= dd, tab_ref[b, head] * LOG2E, val)
            prev = b
    return val


def _bias_kernel(tab_ref, bp_ref, bs_ref, bn_ref, *, n_prompt, t, n_sample, tc, past, ts, bucket_table):
    key = lax.broadcasted_iota(jnp.int32, (t, t), 0)
    qry = lax.broadcasted_iota(jnp.int32, (t, t), 1)
    visible = (key // CHUNK) <= (qry // CHUNK)
    n_cache_tiles = past // tc
    for h in range(N_HEADS):
        for dl in range(n_prompt):
            tile = _bias_tile(tab_ref, h, -dl * t, t, t, bucket_table, keys_on_rows=True)
            if dl == 0:
                tile = jnp.where(visible, tile, NEG_INF)
            bp_ref[h, dl] = tile
        for i in range(n_sample):
            j = n_cache_tiles - n_sample + i
            bs_ref[h, i] = _bias_tile(tab_ref, h, j * tc - past, ts, tc, bucket_table)
        bn_ref[h] = _bias_tile(tab_ref, h, 0, ts, ts, bucket_table)


def _far_distance(bucket_table):
    n = len(bucket_table) - 1
    while n > 0 and bucket_table[n - 1] == bucket_table[-1]:
        n -= 1
    return n


def _num_varying_tiles(tile, bucket_table):
    far = _far_distance(bucket_table)
    n = 1
    while (n - 1) * tile + 1 < far:
        n += 1
    return n


def _gelu_tanh(x):
    c = math.sqrt(2.0 / math.pi)
    return 0.5 * x * (1.0 + jnp.tanh(c * (x + 0.044715 * (x * x * x))))


def _layer_norm(x, g, b):
    mu = jnp.mean(x, axis=-1, keepdims=True)
    xc = x - mu
    var = jnp.mean(xc * xc, axis=-1, keepdims=True)
    return xc * lax.rsqrt(var + LN_EPS) * g + b


def _store_heads(ref, x, n):
    rows = x.shape[0]
    for h in range(n):
        ref[pl.ds(h, rows, stride=n), :] = x[:, h * 128:(h + 1) * 128]


def _proj_kernel(x_ref, w_ref, lng_ref, lnb_ref, ws_ref, bsp_ref, *out_refs, chunk, att_tile, sample):
    if sample:
        q_ref, kf_ref, vf_ref, kb_ref, vb_ref, gm_ref, gn_ref = out_refs
    else:
        q_ref, kf_ref, vf_ref, kb_ref, vt_ref, gm_ref = out_refs
    tm = x_ref.shape[0]
    xb = x_ref[...].astype(BF16)

    def seg(lo, hi):
        return jnp.dot(xb, w_ref[:, lo:hi], preferred_element_type=F32)

    q_ref[...] = (seg(0, D_ATT) * (QK_SCALE * LOG2E)).astype(BF16)
    k = seg(D_ATT, 2 * D_ATT)
    _store_heads(kf_ref, k, N_HEADS)
    kb_ref[...] = k.astype(BF16)
    v = seg(2 * D_ATT, 3 * D_ATT)
    _store_heads(vf_ref, v, N_HEADS)
    if sample:
        vb_ref[...] = v.astype(BF16)
    else:
        ones = jnp.ones((ONES_ROWS, att_tile), F32)
        for j in range(tm // att_tile):
            vt = v[j * att_tile:(j + 1) * att_tile, :].T
            vt_ref[j] = jnp.concatenate(
                [part for h in range(N_HEADS) for part in (vt[h * D_V:(h + 1) * D_V], ones)],
                axis=0).astype(BF16)
    u = _gelu_tanh(seg(3 * D_ATT, 3 * D_ATT + D_GMLP))
    g = _gelu_tanh(seg(3 * D_ATT + D_GMLP, 3 * D_ATT + 2 * D_GMLP))

    lng = lng_ref[...]
    lnb = lnb_ref[...]
    gn = jnp.concatenate(
        [_layer_norm(g[:, i * GROUP_DIM:(i + 1) * GROUP_DIM],
                     lng[:, i * GROUP_DIM:(i + 1) * GROUP_DIM],
                     lnb[:, i * GROUP_DIM:(i + 1) * GROUP_DIM]) for i in range(N_GROUPS)], axis=1)
    if sample:
        _store_heads(gn_ref, gn, N_GROUPS)
    gnb = gn.astype(BF16)

    ri = lax.broadcasted_iota(jnp.int32, (chunk, chunk), 0)
    ci = lax.broadcasted_iota(jnp.int32, (chunk, chunk), 1)
    keep = (ri // CHUNK) >= (ci // CHUNK)
    ws = [jnp.where(keep, ws_ref[i, :chunk, :chunk], 0.0).astype(BF16) for i in range(N_GROUPS)]
    bsp = bsp_ref[...]
    for c in range(tm // chunk):
        r0 = c * chunk
        sp = jnp.concatenate(
            [jnp.dot(ws[i], gnb[r0:r0 + chunk, i * GROUP_DIM:(i + 1) * GROUP_DIM],
                     preferred_element_type=F32) for i in range(N_GROUPS)], axis=1)
        gm_ref[r0:r0 + chunk, :] = (u[r0:r0 + chunk, :] * (sp + bsp)).astype(BF16)


def _diff_lambda(lamv_ref, lam_init):
    lv = lamv_ref[...]
    s1 = jnp.sum(lv[0:1, :] * lv[1:2, :], axis=-1, keepdims=True)
    s2 = jnp.sum(lv[2:3, :] * lv[3:4, :], axis=-1, keepdims=True)
    return jnp.exp(s1) - jnp.exp(s2) + lam_init


def _stack_halves(qh):
    lane = lax.broadcasted_iota(jnp.int32, qh.shape, 1)
    qf = qh.astype(F32)
    first = jnp.where(lane < D_QK, qf, 0.0)
    second = jnp.where(lane >= D_QK, qf, 0.0)
    return jnp.concatenate([first, second], axis=0).astype(BF16)


def _attn_prompt_kernel(lamv_ref, gaint_ref, q_ref, k_ref, vt_ref, bias_ref, o_ref,
                        qz_ref, acc_ref, m_even, m_odd, s_even, s_odd, *, t, n_bias, lam_init):
    qi = pl.program_id(0)
    bb = pl.program_id(1)
    nb = pl.num_programs(1) - 1
    n_tiles = qi + 1
    has1 = bb < nb
    has2 = bb >= 1

    def pass1_head(s_ref, m_ref, h, kj, width):
        off = pl.multiple_of(kj * t, t)
        s = lax.dot_general(k_ref[pl.ds(off, width * t), h * D_V:(h + 1) * D_V], qz_ref[h], _NT,
                            preferred_element_type=F32)
        tiles = [bias_ref[h, jnp.minimum(qi - (kj + i), n_bias - 1)] for i in range(width)]
        s = s + jnp.concatenate([jnp.concatenate([b, b], axis=1) for b in tiles], axis=0)
        s_ref[h, pl.ds(off, width * t), :] = s
        m_ref[h] = jnp.maximum(m_ref[h], jnp.max(s, axis=0, keepdims=True))

    def pass2_head(s_ref, m_ref, h, kj, width):
        off = pl.multiple_of(kj * t, t)
        p = jnp.exp2(s_ref[h, pl.ds(off, width * t), :] - m_ref[h])
        vt = jnp.concatenate([vt_ref[kj + i, h * VT_ROWS:(h + 1) * VT_ROWS, :] for i in range(width)],
                             axis=1)
        acc_ref[h] = acc_ref[h] + jnp.dot(vt, p.astype(BF16), preferred_element_type=F32)

    def over_key_tiles(per_head_steps):
        def tiles(kj, width):
            for h in range(N_HEADS):
                for step in per_head_steps:
                    step(h, kj, width)

        def pair(i, carry):
            tiles(2 * i, 2)
            return carry

        lax.fori_loop(0, n_tiles // 2, pair, 0)

        @pl.when(n_tiles % 2 == 1)
        def _():
            tiles(n_tiles - 1, 1)

    def step(s_new, m_new, s_old, m_old):
        p1 = functools.partial(pass1_head, s_new, m_new)
        p2 = functools.partial(pass2_head, s_old, m_old)

        @pl.when(has1)
        def _():
            for h in range(N_HEADS):
                qz_ref[h] = _stack_halves(q_ref[:, h * D_V:(h + 1) * D_V])
            m_new[...] = jnp.full(m_new.shape, -jnp.inf, F32)

        @pl.when(jnp.logical_and(has1, has2))
        def _():
            over_key_tiles((p1, p2))

        @pl.when(jnp.logical_not(has2))
        def _():
            over_key_tiles((p1,))

        @pl.when(jnp.logical_not(has1))
        def _():
            over_key_tiles((p2,))

    @pl.when(has2)
    def _():
        acc_ref[...] = jnp.zeros(acc_ref.shape, F32)

    @pl.when(jnp.logical_not(has2))
    def _():
        o_ref[...] = jnp.zeros(o_ref.shape, o_ref.dtype)

    @pl.when(bb % 2 == 0)
    def _():
        step(s_even, m_even, s_odd, m_odd)

    @pl.when(bb % 2 == 1)
    def _():
        step(s_odd, m_odd, s_even, m_even)

    @pl.when(has2)
    def _():
        lam = _diff_lambda(lamv_ref, lam_init)
        for h in range(N_HEADS):
            acc = acc_ref[h]
            o = acc[:D_V] / acc[D_V:D_V + 1]
            od = o[:, :t] - lam * o[:, t:]
            ms = jnp.mean(od * od, axis=0, keepdims=True)
            y = od * lax.rsqrt(ms + LN_EPS) * gaint_ref[:, h:h + 1] * (1.0 - lam_init)
            o_ref[:, h * D_V:(h + 1) * D_V] = y.T.astype(BF16)


def _attn_sample_kernel(lamv_ref, gain_ref, q_ref, kn_ref, vn_ref, kc_ref, vc_ref, bias_ref, biasn_ref,
                        o_ref, qz_ref, m_ref, acc_ref, s_ref, sn_ref, *, n_varying, lam_init):
    j = pl.program_id(1)
    nk = pl.num_programs(1) // 2
    ts = q_ref.shape[0]
    tc = kc_ref.shape[0] // N_HEADS

    @pl.when(j == 0)
    def _():
        for h in range(N_HEADS):
            qz_ref[h] = _stack_halves(q_ref[:, h * D_V:(h + 1) * D_V])
        m_ref[...] = jnp.full(m_ref.shape, -jnp.inf, F32)
        acc_ref[...] = jnp.zeros(acc_ref.shape, F32)

    @pl.when(j < nk)
    def _():
        bi = jnp.maximum(j - (nk - n_varying - 1), 0)
        for h in range(N_HEADS):
            kh = kc_ref[pl.ds(h, tc, stride=N_HEADS), :].astype(BF16)
            s = lax.dot_general(qz_ref[h], kh, _NT, preferred_element_type=F32)
            bias = bias_ref[h, bi]
            s = s + jnp.concatenate([bias, bias], axis=0)
            s_ref[h, j] = s
            m_ref[h] = jnp.maximum(m_ref[h], jnp.max(s, axis=-1, keepdims=True))

    @pl.when(j == nk - 1)
    def _():
        for h in range(N_HEADS):
            s = lax.dot_general(qz_ref[h], kn_ref[:, h * D_V:(h + 1) * D_V], _NT,
                                preferred_element_type=F32)
            bias = biasn_ref[h]
            s = s + jnp.concatenate([bias, bias], axis=0)
            sn_ref[h] = s
            m_ref[h] = jnp.maximum(m_ref[h], jnp.max(s, axis=-1, keepdims=True))

    def weighted_values(p, v):
        v1 = jnp.concatenate([v, jnp.ones(v.shape, BF16)], axis=1)
        return jnp.dot(p.astype(BF16), v1, preferred_element_type=F32)

    @pl.when(j >= nk)
    def _():
        for h in range(N_HEADS):
            p = jnp.exp2(s_ref[h, j - nk] - m_ref[h])
            vh = vc_ref[pl.ds(h, tc, stride=N_HEADS), :].astype(BF16)
            acc_ref[h] = acc_ref[h] + weighted_values(p, vh)

    @pl.when(j == 2 * nk - 1)
    def _():
        lam = _diff_lambda(lamv_ref, lam_init)
        for h in range(N_HEADS):
            hs = slice(h * D_V, (h + 1) * D_V)
            p = jnp.exp2(sn_ref[h] - m_ref[h])
            acc = acc_ref[h] + weighted_values(p, vn_ref[:, hs])
            o = acc[:, :D_V] / acc[:, D_V:D_V + 1]
            od = o[:ts] - lam * o[ts:]
            ms = jnp.mean(od * od, axis=-1, keepdims=True)
            y = od * lax.rsqrt(ms + LN_EPS) * gain_ref[h:h + 1, :] * (1.0 - lam_init)
            o_ref[:, hs] = y.astype(BF16)


def _finish_kernel(x_ref, a_ref, gm_ref, wo_ref, g1_ref, b1_ref, wu_ref, wd_ref, g2_ref, b2_ref, y_ref,
                   *, alpha):
    mixed = jnp.concatenate([a_ref[...], gm_ref[...]], axis=1)
    r = jnp.dot(mixed, wo_ref[...], preferred_element_type=F32)
    x1 = _layer_norm(alpha * x_ref[...] + r, g1_ref[...], b1_ref[...])
    x1b = x1.astype(BF16)
    d_ff = wu_ref.shape[1]
    f = jnp.zeros(x1.shape, F32)
    for c in range(d_ff // FFN_CHUNK):
        cs = slice(c * FFN_CHUNK, (c + 1) * FFN_CHUNK)
        hc = jnp.dot(x1b, wu_ref[:, cs], preferred_element_type=F32)
        hc = jnp.square(jnp.maximum(hc, 0.0)).astype(BF16)
        f = f + jnp.dot(hc, wd_ref[cs, :], preferred_element_type=F32)
    y_ref[...] = _layer_norm(alpha * x1 + f, g2_ref[...], b2_ref[...])


def _const_spec(shape):
    nd = len(shape)
    return pl.BlockSpec(shape, lambda *_: (0,) * nd, pipeline_mode=pl.Buffered(1))


def _params(*sem):
    return pltpu.CompilerParams(dimension_semantics=sem, vmem_limit_bytes=VMEM_LIMIT)


def _project(x2d, w_in, lng, lnb, w_spatial, bsp, *, chunk, att_tile, sample):
    n, d = x2d.shape
    tm = min(ROW_TILE, n)
    row = lambda width: pl.BlockSpec((tm, width), lambda i: (i, 0))
    per_head = pl.BlockSpec((tm * N_HEADS, D_V), lambda i: (i, 0))
    out_shape = [jax.ShapeDtypeStruct((n, D_ATT), BF16), jax.ShapeDtypeStruct((n * N_HEADS, D_V), F32),
                 jax.ShapeDtypeStruct((n * N_HEADS, D_V), F32), jax.ShapeDtypeStruct((n, D_ATT), BF16)]
    out_specs = [row(D_ATT), per_head, per_head, row(D_ATT)]
    if sample:
        out_shape += [jax.ShapeDtypeStruct((n, D_ATT), BF16), jax.ShapeDtypeStruct((n, D_GMLP), BF16),
                      jax.ShapeDtypeStruct((n * N_GROUPS, GROUP_DIM), F32)]
        out_specs += [row(D_ATT), row(D_GMLP), pl.BlockSpec((tm * N_GROUPS, GROUP_DIM), lambda i: (i, 0))]
    else:
        out_shape += [jax.ShapeDtypeStruct((n // att_tile, N_HEADS * VT_ROWS, att_tile), BF16),
                      jax.ShapeDtypeStruct((n, D_GMLP), BF16)]
        out_specs += [pl.BlockSpec((tm // att_tile, N_HEADS * VT_ROWS, att_tile), lambda i: (i, 0, 0)), row(D_GMLP)]
    return pl.pallas_call(
        functools.partial(_proj_kernel, chunk=chunk, att_tile=att_tile, sample=sample),
        grid=(n // tm,),
        in_specs=[row(d), _const_spec(w_in.shape), _const_spec(lng.shape), _const_spec(lnb.shape),
                  _const_spec(w_spatial.shape), _const_spec(bsp.shape)],
        out_specs=out_specs,
        out_shape=out_shape,
        compiler_params=_params("parallel"),
        name="proj_sample" if sample else "proj_prompt",
    )(x2d, w_in, lng, lnb, w_spatial, bsp)


def _finish(x2d, attn, gm, w_out, g1, b1, w_up, w_down, g2, b2, *, alpha, name):
    n, d = x2d.shape
    tm = min(ROW_TILE, n)
    row = lambda width: pl.BlockSpec((tm, width), lambda i: (i, 0))
    return pl.pallas_call(
        functools.partial(_finish_kernel, alpha=alpha),
        grid=(n // tm,),
        in_specs=[row(d), row(D_ATT), row(D_GMLP), _const_spec(w_out.shape), _const_spec(g1.shape),
                  _const_spec(b1.shape), _const_spec(w_up.shape), _const_spec(w_down.shape),
                  _const_spec(g2.shape), _const_spec(b2.shape)],
        out_specs=row(d),
        out_shape=jax.ShapeDtypeStruct((n, d), F32),
        compiler_params=_params("parallel"),
        name=name,
    )(x2d, attn, gm, w_out, g1, b1, w_up, w_down, g2, b2)


def kernel(x_prompt, x_sample, cache_k, cache_v, rel_bias_table, w_in, lambda_q1, lambda_k1, lambda_q2,
           lambda_k2, subln_gain, gmlp_ln_gain, gmlp_ln_bias, w_spatial, b_spatial, w_out, ln1_gain,
           ln1_bias, w_ffn_up, w_ffn_down, ln2_gain, ln2_bias):
    depth = w_in.shape[0]
    assert depth == 1, "single-layer step"
    b_p, t_p, d_model = x_prompt.shape
    b_s, t_s, _ = x_sample.shape
    past = cache_k.shape[2]
    alpha = (2.0 * depth) ** 0.25
    lam_init = _lambda_init(0)
    t = ATT_TILE
    tc = CACHE_TILE
    assert t_p % t == 0 and past % tc == 0 and t % CHUNK == 0 and t_s == CHUNK and past % GMLP_CHUNK == 0
    assert ROW_TILE % t == 0 and (b_p * t_p) % ROW_TILE == 0 and (b_s * t_s) % ROW_TILE == 0

    bucket_table = _bucket_of_distance(past + t_s)
    n_prompt_bias = _num_varying_tiles(t, bucket_table) + 1
    n_sample_var = _num_varying_tiles(tc, bucket_table) - 1
    n_sample_bias = n_sample_var + 1
    assert past // tc > n_sample_var

    bias_p, bias_s, bias_n = pl.pallas_call(
        functools.partial(_bias_kernel, n_prompt=n_prompt_bias, t=t, n_sample=n_sample_bias, tc=tc,
                          past=past, ts=t_s, bucket_table=bucket_table),
        in_specs=[pl.BlockSpec(memory_space=pltpu.SMEM)],
        out_shape=[jax.ShapeDtypeStruct((N_HEADS, n_prompt_bias, t, t), F32),
                   jax.ShapeDtypeStruct((N_HEADS, n_sample_bias, t_s, tc), F32),
                   jax.ShapeDtypeStruct((N_HEADS, t_s, t_s), F32)],
        compiler_params=pltpu.CompilerParams(vmem_limit_bytes=VMEM_LIMIT),
        name="rel_bias_tiles",
    )(rel_bias_table)

    w_in_b = w_in[0].astype(BF16)
    w_out_b = w_out[0].astype(BF16)
    w_up_b = w_ffn_up[0].astype(BF16)
    w_down_b = w_ffn_down[0].astype(BF16)
    lng = gmlp_ln_gain[0].reshape(1, D_GMLP)
    lnb = gmlp_ln_bias[0].reshape(1, D_GMLP)
    bsp = jnp.repeat(b_spatial[0].T, GROUP_DIM, axis=1)
    g1, b1 = ln1_gain[0].reshape(1, d_model), ln1_bias[0].reshape(1, d_model)
    g2, b2 = ln2_gain[0].reshape(1, d_model), ln2_bias[0].reshape(1, d_model)
    lamv = jnp.stack([lambda_q1[0], lambda_k1[0], lambda_q2[0], lambda_k2[0]], axis=0)
    gain = subln_gain[0]

    xp = x_prompt.reshape(b_p * t_p, d_model)
    q_p, k_p, v_p, kb_p, vt_p, gm_p = _project(xp, w_in_b, lng, lnb, w_spatial[0], bsp,
                                               chunk=GMLP_CHUNK, att_tile=t, sample=False)
    nq = t_p // t
    gain_t = gain.T
    attn_p = pl.pallas_call(
        functools.partial(_attn_prompt_kernel, t=t, n_bias=n_prompt_bias, lam_init=lam_init),
        grid=(nq, b_p + 1),
        in_specs=[_const_spec(lamv.shape), _const_spec(gain_t.shape),
                  pl.BlockSpec((t, D_ATT), lambda i, bb: (jnp.minimum(bb, b_p - 1) * nq + i, 0)),
                  pl.BlockSpec((t_p, D_ATT), lambda i, bb: (jnp.minimum(bb, b_p - 1), 0)),
                  pl.BlockSpec((nq, N_HEADS * VT_ROWS, t), lambda i, bb: (jnp.maximum(bb - 1, 0), 0, 0)),
                  _const_spec(bias_p.shape)],
        out_specs=pl.BlockSpec((t, D_ATT), lambda i, bb: (jnp.maximum(bb - 1, 0) * nq + i, 0)),
        out_shape=jax.ShapeDtypeStruct((b_p * t_p, D_ATT), BF16),
        scratch_shapes=[pltpu.VMEM((N_HEADS, 2 * t, D_V), BF16), pltpu.VMEM((N_HEADS, VT_ROWS, 2 * t), F32),
                        pltpu.VMEM((N_HEADS, 1, 2 * t), F32), pltpu.VMEM((N_HEADS, 1, 2 * t), F32),
                        pltpu.VMEM((N_HEADS, t_p, 2 * t), F32), pltpu.VMEM((N_HEADS, t_p, 2 * t), F32)],
        compiler_params=_params("arbitrary", "arbitrary"),
        name="attn_prompt",
    )(lamv, gain_t, q_p, kb_p, vt_p, bias_p)
    y_p = _finish(xp, attn_p, gm_p, w_out_b, g1, b1, w_up_b, w_down_b, g2, b2, alpha=alpha,
                  name="finish_prompt")

    xs = x_sample.reshape(b_s * t_s, d_model)
    q_s, k_s, v_s, kb_s, vb_s, gm_s, gn_s = _project(xs, w_in_b, lng, lnb, w_spatial[0], bsp[:t_s],
                                                     chunk=t_s, att_tile=t, sample=True)
    nc = past // tc
    ck = cache_k.reshape(b_s, past * N_HEADS, D_V)
    cv = cache_v.reshape(b_s, past * N_HEADS, D_V)
    key_spec = pl.BlockSpec((None, tc * N_HEADS, D_V), lambda b, j: (b, jnp.minimum(j, nc - 1), 0))
    val_spec = pl.BlockSpec((None, tc * N_HEADS, D_V), lambda b, j: (b, jnp.maximum(j - nc, 0), 0))
    new_spec = pl.BlockSpec((t_s, D_ATT), lambda b, j: (b, 0))
    attn_s = pl.pallas_call(
        functools.partial(_attn_sample_kernel, n_varying=n_sample_var, lam_init=lam_init),
        grid=(b_s, 2 * nc),
        in_specs=[_const_spec(lamv.shape), _const_spec(gain.shape), new_spec, new_spec, new_spec,
                  key_spec, val_spec, _const_spec(bias_s.shape), _const_spec(bias_n.shape)],
        out_specs=new_spec,
        out_shape=jax.ShapeDtypeStruct((b_s * t_s, D_ATT), BF16),
        scratch_shapes=[pltpu.VMEM((N_HEADS, 2 * t_s, D_V), BF16), pltpu.VMEM((N_HEADS, 2 * t_s, 1), F32),
                        pltpu.VMEM((N_HEADS, 2 * t_s, 2 * D_V), F32),
                        pltpu.VMEM((N_HEADS, nc, 2 * t_s, tc), F32),
                        pltpu.VMEM((N_HEADS, 2 * t_s, t_s), F32)],
        compiler_params=_params("parallel", "arbitrary"),
        name="attn_sample",
    )(lamv, gain, q_s, kb_s, vb_s, ck, cv, bias_s, bias_n)
    y_s = _finish(xs, attn_s, gm_s, w_out_b, g1, b1, w_up_b, w_down_b, g2, b2, alpha=alpha,
                  name="finish_sample")

    head_shape_p = (depth, b_p, t_p, N_HEADS, D_V)
    head_shape_s = (depth, b_s, t_s, N_HEADS, D_V)
    return (y_p.reshape(b_p, t_p, d_model), y_s.reshape(b_s, t_s, d_model),
            k_p.reshape(head_shape_p), v_p.reshape(head_shape_p),
            k_s.reshape(head_shape_s), v_s.reshape(head_shape_s),
            gn_s.reshape(depth, b_s, t_s, N_GROUPS, GROUP_DIM))
```

```python
import functools
import math

import numpy as np
import jax
import jax.numpy as jnp
from jax import lax
from jax.experimental import pallas as pl
from jax.experimental.pallas import tpu as pltpu

F32 = jnp.float32
BF16 = jnp.bfloat16

CHUNK = 64
N_HEADS = 4
D_QK = 64
D_V = 2 * D_QK
D_ATT = N_HEADS * D_V
N_GROUPS = 4
GROUP_DIM = 128
D_GMLP = N_GROUPS * GROUP_DIM
GMLP_CHUNK = 128
NUM_BUCKETS = 32
MAX_DISTANCE = 1024
LN_EPS = 1e-5
NEG_INF = -1e30
QK_SCALE = D_QK ** -0.5
LOG2E = math.log2(math.e)
ONES_ROWS = 16
VT_ROWS = D_V + ONES_ROWS

ROW_TILE = 512
ATT_TILE = 256
CACHE_TILE = 2048
FFN_CHUNK = 1024
VMEM_LIMIT = 56 * 1024 * 1024

_NT = (((1,), (1,)), ((), ()))


def _lambda_init(layer):
    return 0.8 - 0.6 * math.exp(-0.3 * layer)


def _bucket_of_distance(max_n):
    half = NUM_BUCKETS // 2
    max_exact = half // 2
    n = np.arange(max_n + 1)

    def table(dtype):
        nf = np.maximum(n, 1).astype(dtype)
        scaled = np.log(nf / dtype(max_exact)) / dtype(math.log(MAX_DISTANCE / max_exact)) * dtype(half - max_exact)
        large = np.minimum(max_exact + scaled.astype(np.int64), half - 1)
        return np.where(n < max_exact, n, large)

    t64 = table(np.float64)
    assert np.array_equal(t64, table(np.float32)), "bucket boundaries are precision sensitive"
    return t64


def _rel_bucket_static(rel, bucket_table):
    return (NUM_BUCKETS // 2 if rel > 0 else 0) + int(bucket_table[abs(rel)])


def _bias_tile(tab_ref, head, d0, nrows, ncols, bucket_table, *, keys_on_rows=False):
    row = lax.broadcasted_iota(jnp.int32, (nrows, ncols), 0)
    col = lax.broadcasted_iota(jnp.int32, (nrows, ncols), 1)
    if keys_on_rows:
        d = (row - col) + d0
        dmin, dmax = d0 - (ncols - 1), d0 + nrows - 1
    else:
        d = (col - row) + d0
        dmin, dmax = d0 - (nrows - 1), d0 + ncols - 1
    prev = _rel_bucket_static(dmin, bucket_table)
    val = jnp.full((nrows, ncols), tab_ref[prev, head] * LOG2E, F32)
    for dd in range(dmin + 1, dmax + 1):
        b = _rel_bucket_static(dd, bucket_table)
        if b != prev:
            val = jnp.where(d >= dd, tab_ref[b, head] * LOG2E, val)
            prev = b
    return val


def _bias_kernel(tab_ref, bp_ref, bs_ref, bn_ref, *, n_prompt, t, n_sample, tc, past, ts, bucket_table):
    key = lax.broadcasted_iota(jnp.int32, (t, t), 0)
    qry = lax.broadcasted_iota(jnp.int32, (t, t), 1)
    visible = (key // CHUNK) <= (qry // CHUNK)
    n_cache_tiles = past // tc
    for h in range(N_HEADS):
        for dl in range(n_prompt):
            tile = _bias_tile(tab_ref, h, -dl * t, t, t, bucket_table, keys_on_rows=True)
            if dl == 0:
                tile = jnp.where(visible, tile, NEG_INF)
            bp_ref[h, dl] = tile
        for i in range(n_sample):
            j = n_cache_tiles - n_sample + i
            bs_ref[h, i] = _bias_tile(tab_ref, h, j * tc - past, ts, tc, bucket_table)
        bn_ref[h] = _bias_tile(tab_ref, h, 0, ts, ts, bucket_table)


def _far_distance(bucket_table):
    n = len(bucket_table) - 1
    while n > 0 and bucket_table[n - 1] == bucket_table[-1]:
        n -= 1
    return n


def _num_varying_tiles(tile, bucket_table):
    far = _far_distance(bucket_table)
    n = 1
    while (n - 1) * tile + 1 < far:
        n += 1
    return n


def _gelu_tanh(x):
    c = math.sqrt(2.0 / math.pi)
    return 0.5 * x * (1.0 + jnp.tanh(c * (x + 0.044715 * (x * x * x))))


def _layer_norm(x, g, b):
    mu = jnp.mean(x, axis=-1, keepdims=True)
    xc = x - mu
    var = jnp.mean(xc * xc, axis=-1, keepdims=True)
    return xc * lax.rsqrt(var + LN_EPS) * g + b


def _store_heads(ref, x, n):
    rows = x.shape[0]
    for h in range(n):
        ref[pl.ds(h, rows, stride=n), :] = x[:, h * 128:(h + 1) * 128]


def _proj_kernel(x_ref, w_ref, lng_ref, lnb_ref, ws_ref, bsp_ref, *out_refs, chunk, att_tile, sample):
    if sample:
        q_ref, kf_ref, vf_ref, kb_ref, vb_ref, gm_ref, gn_ref = out_refs
    else:
        q_ref, kf_ref, vf_ref, kb_ref, vt_ref, gm_ref = out_refs
    tm = x_ref.shape[0]
    xb = x_ref[...].astype(BF16)

    def seg(lo, hi):
        return jnp.dot(xb, w_ref[:, lo:hi], preferred_element_type=F32)

    g_lin = seg(3 * D_ATT + D_GMLP, 3 * D_ATT + 2 * D_GMLP)
    u_lin = seg(3 * D_ATT, 3 * D_ATT + D_GMLP)

    lng = lng_ref[...]
    lnb = lnb_ref[...]
    g = _gelu_tanh(g_lin)
    gn = jnp.concatenate(
        [_layer_norm(g[:, i * GROUP_DIM:(i + 1) * GROUP_DIM],
                     lng[:, i * GROUP_DIM:(i + 1) * GROUP_DIM],
                     lnb[:, i * GROUP_DIM:(i + 1) * GROUP_DIM]) for i in range(N_GROUPS)], axis=1)
    if sample:
        _store_heads(gn_ref, gn, N_GROUPS)
    gnb = gn.astype(BF16)

    v = seg(2 * D_ATT, 3 * D_ATT)
    _store_heads(vf_ref, v, N_HEADS)
    if sample:
        vb_ref[...] = v.astype(BF16)
    else:
        ones = jnp.ones((ONES_ROWS, att_tile), F32)
        for j in range(tm // att_tile):
            vt = v[j * att_tile:(j + 1) * att_tile, :].T
            vt_ref[j] = jnp.concatenate(
                [part for h in range(N_HEADS) for part in (vt[h * D_V:(h + 1) * D_V], ones)],
                axis=0).astype(BF16)

    u = _gelu_tanh(u_lin)

    k = seg(D_ATT, 2 * D_ATT)
    _store_heads(kf_ref, k, N_HEADS)
    kb_ref[...] = k.astype(BF16)
    q_ref[...] = (seg(0, D_ATT) * (QK_SCALE * LOG2E)).astype(BF16)

    ri = lax.broadcasted_iota(jnp.int32, (chunk, chunk), 0)
    ci = lax.broadcasted_iota(jnp.int32, (chunk, chunk), 1)
    keep = (ri // CHUNK) >= (ci // CHUNK)
    ws = [jnp.where(keep, ws_ref[i, :chunk, :chunk], 0.0).astype(BF16) for i in range(N_GROUPS)]
    bsp = bsp_ref[...]
    for c in range(tm // chunk):
        r0 = c * chunk
        sp = jnp.concatenate(
            [jnp.dot(ws[i], gnb[r0:r0 + chunk, i * GROUP_DIM:(i + 1) * GROUP_DIM],
                     preferred_element_type=F32) for i in range(N_GROUPS)], axis=1)
        gm_ref[r0:r0 + chunk, :] = (u[r0:r0 + chunk, :] * (sp + bsp)).astype(BF16)


def _diff_lambda(lamv_ref, lam_init):
    lv = lamv_ref[...]
    s1 = jnp.sum(lv[0:1, :] * lv[1:2, :], axis=-1, keepdims=True)
    s2 = jnp.sum(lv[2:3, :] * lv[3:4, :], axis=-1, keepdims=True)
    return jnp.exp(s1) - jnp.exp(s2) + lam_init


def _stack_halves(qh):
    lane = lax.broadcasted_iota(jnp.int32, qh.shape, 1)
    qf = qh.astype(F32)
    first = jnp.where(lane < D_QK, qf, 0.0)
    second = jnp.where(lane >= D_QK, qf, 0.0)
    return jnp.concatenate([first, second], axis=0).astype(BF16)


def _attn_prompt_kernel(lamv_ref, gaint_ref, q_ref, k_ref, vt_ref, bias_ref, o_ref,
                        qz_ref, acc_ref, m_even, m_odd, s_even, s_odd, *, t, n_bias, lam_init):
    qi = pl.program_id(0)
    bb = pl.program_id(1)
    nb = pl.num_programs(1) - 1
    n_tiles = qi + 1
    has1 = bb < nb
    has2 = bb >= 1

    def pass1_head(s_ref, m_ref, h, kj, width):
        off = pl.multiple_of(kj * t, t)
        s = lax.dot_general(k_ref[pl.ds(off, width * t), h * D_V:(h + 1) * D_V], qz_ref[h], _NT,
                            preferred_element_type=F32)
        tiles = [bias_ref[h, jnp.minimum(qi - (kj + i), n_bias - 1)] for i in range(width)]
        s = s + jnp.concatenate([jnp.concatenate([b, b], axis=1) for b in tiles], axis=0)
        s_ref[h, pl.ds(off, width * t), :] = s
        m_ref[h] = jnp.maximum(m_ref[h], jnp.max(s, axis=0, keepdims=True))

    def pass2_head(s_ref, m_ref, h, kj, width):
        off = pl.multiple_of(kj * t, t)
        p = jnp.exp2(s_ref[h, pl.ds(off, width * t), :] - m_ref[h])
        vt = jnp.concatenate([vt_ref[kj + i, h * VT_ROWS:(h + 1) * VT_ROWS, :] for i in range(width)],
                             axis=1)
        acc_ref[h] = acc_ref[h] + jnp.dot(vt, p.astype(BF16), preferred_element_type=F32)

    def over_key_tiles(per_head_steps):
        def tiles(kj, width):
            for h in range(N_HEADS):
                for step in per_head_steps:
                    step(h, kj, width)

        def pair(i, carry):
            tiles(2 * i, 2)
            return carry

        lax.fori_loop(0, n_tiles // 2, pair, 0)

        @pl.when(n_tiles % 2 == 1)
        def _():
            tiles(n_tiles - 1, 1)

    def step(s_new, m_new, s_old, m_old):
        p1 = functools.partial(pass1_head, s_new, m_new)
        p2 = functools.partial(pass2_head, s_old, m_old)

        @pl.when(has1)
        def _():
            for h in range(N_HEADS):
                qz_ref[h] = _stack_halves(q_ref[:, h * D_V:(h + 1) * D_V])
            m_new[...] = jnp.full(m_new.shape, -jnp.inf, F32)

        @pl.when(jnp.logical_and(has1, has2))
        def _():
            over_key_tiles((p1, p2))

        @pl.when(jnp.logical_not(has2))
        def _():
            over_key_tiles((p1,))

        @pl.when(jnp.logical_not(has1))
        def _():
            over_key_tiles((p2,))

    @pl.when(has2)
    def _():
        acc_ref[...] = jnp.zeros(acc_ref.shape, F32)

    @pl.when(jnp.logical_not(has2))
    def _():
        o_ref[...] = jnp.zeros(o_ref.shape, o_ref.dtype)

    @pl.when(bb % 2 == 0)
    def _():
        step(s_even, m_even, s_odd, m_odd)

    @pl.when(bb % 2 == 1)
    def _():
        step(s_odd, m_odd, s_even, m_even)

    @pl.when(has2)
    def _():
        lam = _diff_lambda(lamv_ref, lam_init)
        for h in range(N_HEADS):
            acc = acc_ref[h]
            o = acc[:D_V] / acc[D_V:D_V + 1]
            od = o[:, :t] - lam * o[:, t:]
            ms = jnp.mean(od * od, axis=0, keepdims=True)
            y = od * lax.rsqrt(ms + LN_EPS) * gaint_ref[:, h:h + 1] * (1.0 - lam_init)
            o_ref[:, h * D_V:(h + 1) * D_V] = y.T.astype(BF16)


def _attn_sample_kernel(lamv_ref, gain_ref, q_ref, kn_ref, vn_ref, kc_ref, vc_ref, bias_ref, biasn_ref,
                        o_ref, qz_ref, m_ref, acc_ref, s_ref, sn_ref, *, n_varying, lam_init):
    j = pl.program_id(1)
    nk = pl.num_programs(1) // 2
    ts = q_ref.shape[0]
    tc = kc_ref.shape[0] // N_HEADS

    @pl.when(j == 0)
    def _():
        for h in range(N_HEADS):
            qz_ref[h] = _stack_halves(q_ref[:, h * D_V:(h + 1) * D_V])
        m_ref[...] = jnp.full(m_ref.shape, -jnp.inf, F32)
        acc_ref[...] = jnp.zeros(acc_ref.shape, F32)

    @pl.when(j < nk)
    def _():
        bi = jnp.maximum(j - (nk - n_varying - 1), 0)
        for h in range(N_HEADS):
            kh = kc_ref[pl.ds(h, tc, stride=N_HEADS), :].astype(BF16)
            s = lax.dot_general(qz_ref[h], kh, _NT, preferred_element_type=F32)
            bias = bias_ref[h, bi]
            s = s + jnp.concatenate([bias, bias], axis=0)
            s_ref[h, j] = s
            m_ref[h] = jnp.maximum(m_ref[h], jnp.max(s, axis=-1, keepdims=True))

    @pl.when(j == nk - 1)
    def _():
        for h in range(N_HEADS):
            s = lax.dot_general(qz_ref[h], kn_ref[:, h * D_V:(h + 1) * D_V], _NT,
                                preferred_element_type=F32)
            bias = biasn_ref[h]
            s = s + jnp.concatenate([bias, bias], axis=0)
            sn_ref[h] = s
            m_ref[h] = jnp.maximum(m_ref[h], jnp.max(s, axis=-1, keepdims=True))

    def weighted_values(p, v):
        v1 = jnp.concatenate([v, jnp.ones(v.shape, BF16)], axis=1)
        return jnp.dot(p.astype(BF16), v1, preferred_element_type=F32)

    @pl.when(j >= nk)
    def _():
        for h in range(N_HEADS):
            p = jnp.exp2(s_ref[h, j - nk] - m_ref[h])
            vh = vc_ref[pl.ds(h, tc, stride=N_HEADS), :].astype(BF16)
            acc_ref[h] = acc_ref[h] + weighted_values(p, vh)

    @pl.when(j == 2 * nk - 1)
    def _():
        lam = _diff_lambda(lamv_ref, lam_init)
        for h in range(N_HEADS):
            hs = slice(h * D_V, (h + 1) * D_V)
            p = jnp.exp2(sn_ref[h] - m_ref[h])
            acc = acc_ref[h] + weighted_values(p, vn_ref[:, hs])
            o = acc[:, :D_V] / acc[:, D_V:D_V + 1]
            od = o[:ts] - lam * o[ts:]
            ms = jnp.mean(od * od, axis=-1, keepdims=True)
            y = od * lax.rsqrt(ms + LN_EPS) * gain_ref[h:h + 1, :] * (1.0 - lam_init)
            o_ref[:, hs] = y.astype(BF16)


def _out_proj_ln(x_ref, a_ref, gm_ref, wo_ref, g1_ref, b1_ref, alpha):
    mixed = jnp.concatenate([a_ref[...], gm_ref[...]], axis=1)
    r = jnp.dot(mixed, wo_ref[...], preferred_element_type=F32)
    return _layer_norm(alpha * x_ref[...] + r, g1_ref[...], b1_ref[...])


def _ffn_up(x1b, wu_ref, c):
    hc = jnp.dot(x1b, wu_ref[:, c * FFN_CHUNK:(c + 1) * FFN_CHUNK], preferred_element_type=F32)
    return jnp.square(jnp.maximum(hc, 0.0)).astype(BF16)


def _ffn_down(hc, wd_ref, c):
    return jnp.dot(hc, wd_ref[c * FFN_CHUNK:(c + 1) * FFN_CHUNK, :], preferred_element_type=F32)


def _finish_kernel(x_ref, a_ref, gm_ref, wo_ref, g1_ref, b1_ref, wu_ref, wd_ref, g2_ref, b2_ref, y_ref,
                   *, alpha):
    half = x_ref.shape[0] // 2
    rows = (slice(0, half), slice(half, 2 * half))

    def out_proj_ln(rs):
        mixed = jnp.concatenate([a_ref[rs, :], gm_ref[rs, :]], axis=1)
        r = jnp.dot(mixed, wo_ref[...], preferred_element_type=F32)
        return _layer_norm(alpha * x_ref[rs, :] + r, g1_ref[...], b1_ref[...])

    def ffn(x1):
        x1b = x1.astype(BF16)
        f = alpha * x1
        for c in range(wu_ref.shape[1] // FFN_CHUNK):
            f = f + _ffn_down(_ffn_up(x1b, wu_ref, c), wd_ref, c)
        return f

    x1_a = out_proj_ln(rows[0])
    x1_b = out_proj_ln(rows[1])
    f_a = ffn(x1_a)
    y_ref[rows[0], :] = _layer_norm(f_a, g2_ref[...], b2_ref[...])
    f_b = ffn(x1_b)
    y_ref[rows[1], :] = _layer_norm(f_b, g2_ref[...], b2_ref[...])


def _const_spec(shape):
    nd = len(shape)
    return pl.BlockSpec(shape, lambda *_: (0,) * nd, pipeline_mode=pl.Buffered(1))


def _params(*sem):
    return pltpu.CompilerParams(dimension_semantics=sem, vmem_limit_bytes=VMEM_LIMIT)


def _project(x2d, w_in, lng, lnb, w_spatial, bsp, *, chunk, att_tile, sample):
    n, d = x2d.shape
    tm = min(ROW_TILE, n)
    row = lambda width: pl.BlockSpec((tm, width), lambda i: (i, 0))
    per_head = pl.BlockSpec((tm * N_HEADS, D_V), lambda i: (i, 0))
    out_shape = [jax.ShapeDtypeStruct((n, D_ATT), BF16), jax.ShapeDtypeStruct((n * N_HEADS, D_V), F32),
                 jax.ShapeDtypeStruct((n * N_HEADS, D_V), F32), jax.ShapeDtypeStruct((n, D_ATT), BF16)]
    out_specs = [row(D_ATT), per_head, per_head, row(D_ATT)]
    if sample:
        out_shape += [jax.ShapeDtypeStruct((n, D_ATT), BF16), jax.ShapeDtypeStruct((n, D_GMLP), BF16),
                      jax.ShapeDtypeStruct((n * N_GROUPS, GROUP_DIM), F32)]
        out_specs += [row(D_ATT), row(D_GMLP), pl.BlockSpec((tm * N_GROUPS, GROUP_DIM), lambda i: (i, 0))]
    else:
        out_shape += [jax.ShapeDtypeStruct((n // att_tile, N_HEADS * VT_ROWS, att_tile), BF16),
                      jax.ShapeDtypeStruct((n, D_GMLP), BF16)]
        out_specs += [pl.BlockSpec((tm // att_tile, N_HEADS * VT_ROWS, att_tile), lambda i: (i, 0, 0)), row(D_GMLP)]
    return pl.pallas_call(
        functools.partial(_proj_kernel, chunk=chunk, att_tile=att_tile, sample=sample),
        grid=(n // tm,),
        in_specs=[row(d), _const_spec(w_in.shape), _const_spec(lng.shape), _const_spec(lnb.shape),
                  _const_spec(w_spatial.shape), _const_spec(bsp.shape)],
        out_specs=out_specs,
        out_shape=out_shape,
        compiler_params=_params("parallel"),
        name="proj_sample" if sample else "proj_prompt",
    )(x2d, w_in, lng, lnb, w_spatial, bsp)


def _finish(x2d, attn, gm, w_out, g1, b1, w_up, w_down, g2, b2, *, alpha, name):
    n, d = x2d.shape
    tm = min(ROW_TILE, n)
    row = lambda width: pl.BlockSpec((tm, width), lambda i: (i, 0))
    return pl.pallas_call(
        functools.partial(_finish_kernel, alpha=alpha),
        grid=(n // tm,),
        in_specs=[row(d), row(D_ATT), row(D_GMLP), _const_spec(w_out.shape), _const_spec(g1.shape),
                  _const_spec(b1.shape), _const_spec(w_up.shape), _const_spec(w_down.shape),
                  _const_spec(g2.shape), _const_spec(b2.shape)],
        out_specs=row(d),
        out_shape=jax.ShapeDtypeStruct((n, d), F32),
        compiler_params=_params("parallel"),
        name=name,
    )(x2d, attn, gm, w_out, g1, b1, w_up, w_down, g2, b2)


def kernel(x_prompt, x_sample, cache_k, cache_v, rel_bias_table, w_in, lambda_q1, lambda_k1, lambda_q2,
           lambda_k2, subln_gain, gmlp_ln_gain, gmlp_ln_bias, w_spatial, b_spatial, w_out, ln1_gain,
           ln1_bias, w_ffn_up, w_ffn_down, ln2_gain, ln2_bias):
    depth = w_in.shape[0]
    assert depth == 1, "single-layer step"
    b_p, t_p, d_model = x_prompt.shape
    b_s, t_s, _ = x_sample.shape
    past = cache_k.shape[2]
    alpha = (2.0 * depth) ** 0.25
    lam_init = _lambda_init(0)
    t = ATT_TILE
    tc = CACHE_TILE
    assert t_p % t == 0 and past % tc == 0 and t % CHUNK == 0 and t_s == CHUNK and past % GMLP_CHUNK == 0
    assert ROW_TILE % t == 0 and (b_p * t_p) % ROW_TILE == 0 and (b_s * t_s) % ROW_TILE == 0

    bucket_table = _bucket_of_distance(past + t_s)
    n_prompt_bias = _num_varying_tiles(t, bucket_table) + 1
    n_sample_var = _num_varying_tiles(tc, bucket_table) - 1
    n_sample_bias = n_sample_var + 1
    assert past // tc > n_sample_var

    bias_p, bias_s, bias_n = pl.pallas_call(
        functools.partial(_bias_kernel, n_prompt=n_prompt_bias, t=t, n_sample=n_sample_bias, tc=tc,
                          past=past, ts=t_s, bucket_table=bucket_table),
        in_specs=[pl.BlockSpec(memory_space=pltpu.SMEM)],
        out_shape=[jax.ShapeDtypeStruct((N_HEADS, n_prompt_bias, t, t), F32),
                   jax.ShapeDtypeStruct((N_HEADS, n_sample_bias, t_s, tc), F32),
                   jax.ShapeDtypeStruct((N_HEADS, t_s, t_s), F32)],
        compiler_params=pltpu.CompilerParams(vmem_limit_bytes=VMEM_LIMIT),
        name="rel_bias_tiles",
    )(rel_bias_table)

    w_in_b = w_in[0].astype(BF16)
    w_out_b = w_out[0].astype(BF16)
    w_up_b = w_ffn_up[0].astype(BF16)
    w_down_b = w_ffn_down[0].astype(BF16)
    lng = gmlp_ln_gain[0].reshape(1, D_GMLP)
    lnb = gmlp_ln_bias[0].reshape(1, D_GMLP)
    bsp = jnp.repeat(b_spatial[0].T, GROUP_DIM, axis=1)
    g1, b1 = ln1_gain[0].reshape(1, d_model), ln1_bias[0].reshape(1, d_model)
    g2, b2 = ln2_gain[0].reshape(1, d_model), ln2_bias[0].reshape(1, d_model)
    lamv = jnp.stack([lambda_q1[0], lambda_k1[0], lambda_q2[0], lambda_k2[0]], axis=0)
    gain = subln_gain[0]

    xp = x_prompt.reshape(b_p * t_p, d_model)
    q_p, k_p, v_p, kb_p, vt_p, gm_p = _project(xp, w_in_b, lng, lnb, w_spatial[0], bsp,
                                               chunk=GMLP_CHUNK, att_tile=t, sample=False)
    nq = t_p // t
    gain_t = gain.T
    attn_p = pl.pallas_call(
        functools.partial(_attn_prompt_kernel, t=t, n_bias=n_prompt_bias, lam_init=lam_init),
        grid=(nq, b_p + 1),
        in_specs=[_const_spec(lamv.shape), _const_spec(gain_t.shape),
                  pl.BlockSpec((t, D_ATT), lambda i, bb: (jnp.minimum(bb, b_p - 1) * nq + i, 0)),
                  pl.BlockSpec((t_p, D_ATT), lambda i, bb: (jnp.minimum(bb, b_p - 1), 0)),
                  pl.BlockSpec((nq, N_HEADS * VT_ROWS, t), lambda i, bb: (jnp.maximum(bb - 1, 0), 0, 0)),
                  _const_spec(bias_p.shape)],
        out_specs=pl.BlockSpec((t, D_ATT), lambda i, bb: (jnp.maximum(bb - 1, 0) * nq + i, 0)),
        out_shape=jax.ShapeDtypeStruct((b_p * t_p, D_ATT), BF16),
        scratch_shapes=[pltpu.VMEM((N_HEADS, 2 * t, D_V), BF16), pltpu.VMEM((N_HEADS, VT_ROWS, 2 * t), F32),
                        pltpu.VMEM((N_HEADS, 1, 2 * t), F32), pltpu.VMEM((N_HEADS, 1, 2 * t), F32),
                        pltpu.VMEM((N_HEADS, t_p, 2 * t), F32), pltpu.VMEM((N_HEADS, t_p, 2 * t), F32)],
        compiler_params=_params("arbitrary", "arbitrary"),
        name="attn_prompt",
    )(lamv, gain_t, q_p, kb_p, vt_p, bias_p)
    y_p = _finish(xp, attn_p, gm_p, w_out_b, g1, b1, w_up_b, w_down_b, g2, b2, alpha=alpha,
                  name="finish_prompt")

    xs = x_sample.reshape(b_s * t_s, d_model)
    q_s, k_s, v_s, kb_s, vb_s, gm_s, gn_s = _project(xs, w_in_b, lng, lnb, w_spatial[0], bsp[:t_s],
                                                     chunk=t_s, att_tile=t, sample=True)
    nc = past // tc
    ck = cache_k.reshape(b_s, past * N_HEADS, D_V)
    cv = cache_v.reshape(b_s, past * N_HEADS, D_V)
    key_spec = pl.BlockSpec((None, tc * N_HEADS, D_V), lambda b, j: (b, jnp.minimum(j, nc - 1), 0))
    val_spec = pl.BlockSpec((None, tc * N_HEADS, D_V), lambda b, j: (b, jnp.maximum(j - nc, 0), 0))
    new_spec = pl.BlockSpec((t_s, D_ATT), lambda b, j: (b, 0))
    attn_s = pl.pallas_call(
        functools.partial(_attn_sample_kernel, n_varying=n_sample_var, lam_init=lam_init),
        grid=(b_s, 2 * nc),
        in_specs=[_const_spec(lamv.shape), _const_spec(gain.shape), new_spec, new_spec, new_spec,
                  key_spec, val_spec, _const_spec(bias_s.shape), _const_spec(bias_n.shape)],
        out_specs=new_spec,
        out_shape=jax.ShapeDtypeStruct((b_s * t_s, D_ATT), BF16),
        scratch_shapes=[pltpu.VMEM((N_HEADS, 2 * t_s, D_V), BF16), pltpu.VMEM((N_HEADS, 2 * t_s, 1), F32),
                        pltpu.VMEM((N_HEADS, 2 * t_s, 2 * D_V), F32),
                        pltpu.VMEM((N_HEADS, nc, 2 * t_s, tc), F32),
                        pltpu.VMEM((N_HEADS, 2 * t_s, t_s), F32)],
        compiler_params=_params("parallel", "arbitrary"),
        name="attn_sample",
    )(lamv, gain, q_s, kb_s, vb_s, ck, cv, bias_s, bias_n)
    y_s = _finish(xs, attn_s, gm_s, w_out_b, g1, b1, w_up_b, w_down_b, g2, b2, alpha=alpha,
                  name="finish_sample")

    head_shape_p = (depth, b_p, t_p, N_HEADS, D_V)
    head_shape_s = (depth, b_s, t_s, N_HEADS, D_V)
    return (y_p.reshape(b_p, t_p, d_model), y_s.reshape(b_s, t_s, d_model),
            k_p.reshape(head_shape_p), v_p.reshape(head_shape_p),
            k_s.reshape(head_shape_s), v_s.reshape(head_shape_s),
            gn_s.reshape(depth, b_s, t_s, N_GROUPS, GROUP_DIM))
```

```python
import functools
import math

import numpy as np
import jax
import jax.numpy as jnp
from jax import lax
from jax.experimental import pallas as pl
from jax.experimental.pallas import tpu as pltpu

F32 = jnp.float32
BF16 = jnp.bfloat16

CHUNK = 64
N_HEADS = 4
D_QK = 64
D_V = 2 * D_QK
D_ATT = N_HEADS * D_V
N_GROUPS = 4
GROUP_DIM = 128
D_GMLP = N_GROUPS * GROUP_DIM
GMLP_CHUNK = 128
NUM_BUCKETS = 32
MAX_DISTANCE = 1024
LN_EPS = 1e-5
NEG_INF = -1e30
QK_SCALE = D_QK ** -0.5
LOG2E = math.log2(math.e)
ONES_ROWS = 16
VT_ROWS = D_V + ONES_ROWS

ROW_TILE = 1024
FINISH_ROW_TILE = 1024
FINISH_PART_QUARTERS = (1, 3)
ATT_TILE = 256
KEY_TILES_PER_TRIP = 2
CACHE_TILE = 2048
FFN_CHUNK = 1024
VMEM_LIMIT = 56 * 1024 * 1024

_NT = (((1,), (1,)), ((), ()))


def _lambda_init(layer):
    return 0.8 - 0.6 * math.exp(-0.3 * layer)


def _bucket_of_distance(max_n):
    half = NUM_BUCKETS // 2
    max_exact = half // 2
    n = np.arange(max_n + 1)

    def table(dtype):
        nf = np.maximum(n, 1).astype(dtype)
        scaled = np.log(nf / dtype(max_exact)) / dtype(math.log(MAX_DISTANCE / max_exact)) * dtype(half - max_exact)
        large = np.minimum(max_exact + scaled.astype(np.int64), half - 1)
        return np.where(n < max_exact, n, large)

    t64 = table(np.float64)
    assert np.array_equal(t64, table(np.float32)), "bucket boundaries are precision sensitive"
    return t64


def _rel_bucket_static(rel, bucket_table):
    return (NUM_BUCKETS // 2 if rel > 0 else 0) + int(bucket_table[abs(rel)])


def _bias_tile(tab_ref, head, d0, nrows, ncols, bucket_table, *, keys_on_rows=False):
    row = lax.broadcasted_iota(jnp.int32, (nrows, ncols), 0)
    col = lax.broadcasted_iota(jnp.int32, (nrows, ncols), 1)
    if keys_on_rows:
        d = (row - col) + d0
        dmin, dmax = d0 - (ncols - 1), d0 + nrows - 1
    else:
        d = (col - row) + d0
        dmin, dmax = d0 - (nrows - 1), d0 + ncols - 1
    prev = _rel_bucket_static(dmin, bucket_table)
    val = jnp.full((nrows, ncols), tab_ref[prev, head] * LOG2E, F32)
    for dd in range(dmin + 1, dmax + 1):
        b = _rel_bucket_static(dd, bucket_table)
        if b != prev:
            val = jnp.where(d >= dd, tab_ref[b, head] * LOG2E, val)
            prev = b
    return val


def _bias_kernel(tab_ref, bp_ref, bs_ref, bn_ref, *, n_prompt, t, n_sample, tc, past, ts, bucket_table):
    key = lax.broadcasted_iota(jnp.int32, (t, t), 0)
    qry = lax.broadcasted_iota(jnp.int32, (t, t), 1)
    visible = (key // CHUNK) <= (qry // CHUNK)
    n_cache_tiles = past // tc
    for h in range(N_HEADS):
        for dl in range(n_prompt):
            tile = _bias_tile(tab_ref, h, -dl * t, t, t, bucket_table, keys_on_rows=True)
            if dl == 0:
                tile = jnp.where(visible, tile, NEG_INF)
            bp_ref[h, dl] = tile
        for i in range(n_sample):
            j = n_cache_tiles - n_sample + i
            bs_ref[h, i] = _bias_tile(tab_ref, h, j * tc - past, ts, tc, bucket_table)
        bn_ref[h] = _bias_tile(tab_ref, h, 0, ts, ts, bucket_table)


def _far_distance(bucket_table):
    n = len(bucket_table) - 1
    while n > 0 and bucket_table[n - 1] == bucket_table[-1]:
        n -= 1
    return n


def _num_varying_tiles(tile, bucket_table):
    far = _far_distance(bucket_table)
    n = 1
    while (n - 1) * tile + 1 < far:
        n += 1
    return n


def _gelu_tanh(x):
    c = math.sqrt(2.0 / math.pi)
    return 0.5 * x * (1.0 + jnp.tanh(c * (x + 0.044715 * (x * x * x))))


def _layer_norm(x, g, b):
    mu = jnp.mean(x, axis=-1, keepdims=True)
    xc = x - mu
    var = jnp.mean(xc * xc, axis=-1, keepdims=True)
    return xc * lax.rsqrt(var + LN_EPS) * g + b


def _store_heads(ref, x, n):
    rows = x.shape[0]
    for h in range(n):
        ref[pl.ds(h, rows, stride=n), :] = x[:, h * 128:(h + 1) * 128]


def _proj_kernel(x_ref, w_ref, lng_ref, lnb_ref, ws_ref, bsp_ref, *out_refs, chunk, att_tile, sample):
    if sample:
        q_ref, kf_ref, vf_ref, kb_ref, vb_ref, gm_ref, gn_ref = out_refs
    else:
        q_ref, kf_ref, vf_ref, kb_ref, vt_ref, gm_ref = out_refs
    tm = x_ref.shape[0]
    xb = x_ref[...].astype(BF16)

    def seg(lo, hi):
        return jnp.dot(xb, w_ref[:, lo:hi], preferred_element_type=F32)

    g_lin = seg(3 * D_ATT + D_GMLP, 3 * D_ATT + 2 * D_GMLP)
    u_lin = seg(3 * D_ATT, 3 * D_ATT + D_GMLP)

    lng = lng_ref[...]
    lnb = lnb_ref[...]
    g = _gelu_tanh(g_lin)
    gn = jnp.concatenate(
        [_layer_norm(g[:, i * GROUP_DIM:(i + 1) * GROUP_DIM],
                     lng[:, i * GROUP_DIM:(i + 1) * GROUP_DIM],
                     lnb[:, i * GROUP_DIM:(i + 1) * GROUP_DIM]) for i in range(N_GROUPS)], axis=1)
    if sample:
        _store_heads(gn_ref, gn, N_GROUPS)
    gnb = gn.astype(BF16)

    v = seg(2 * D_ATT, 3 * D_ATT)
    _store_heads(vf_ref, v, N_HEADS)
    if sample:
        vb_ref[...] = v.astype(BF16)
    else:
        ones = jnp.ones((ONES_ROWS, att_tile), F32)
        for j in range(tm // att_tile):
            vt = v[j * att_tile:(j + 1) * att_tile, :].T
            vt_ref[j] = jnp.concatenate(
                [part for h in range(N_HEADS) for part in (vt[h * D_V:(h + 1) * D_V], ones)],
                axis=0).astype(BF16)

    u = _gelu_tanh(u_lin)

    k = seg(D_ATT, 2 * D_ATT)
    _store_heads(kf_ref, k, N_HEADS)
    kb_ref[...] = k.astype(BF16)
    q_ref[...] = (seg(0, D_ATT) * (QK_SCALE * LOG2E)).astype(BF16)

    ri = lax.broadcasted_iota(jnp.int32, (chunk, chunk), 0)
    ci = lax.broadcasted_iota(jnp.int32, (chunk, chunk), 1)
    keep = (ri // CHUNK) >= (ci // CHUNK)
    ws = [jnp.where(keep, ws_ref[i, :chunk, :chunk], 0.0).astype(BF16) for i in range(N_GROUPS)]
    bsp = bsp_ref[...]
    for c in range(tm // chunk):
        r0 = c * chunk
        sp = jnp.concatenate(
            [jnp.dot(ws[i], gnb[r0:r0 + chunk, i * GROUP_DIM:(i + 1) * GROUP_DIM],
                     preferred_element_type=F32) for i in range(N_GROUPS)], axis=1)
        gm_ref[r0:r0 + chunk, :] = (u[r0:r0 + chunk, :] * (sp + bsp)).astype(BF16)


def _diff_lambda(lamv_ref, lam_init):
    lv = lamv_ref[...]
    s1 = jnp.sum(lv[0:1, :] * lv[1:2, :], axis=-1, keepdims=True)
    s2 = jnp.sum(lv[2:3, :] * lv[3:4, :], axis=-1, keepdims=True)
    return jnp.exp(s1) - jnp.exp(s2) + lam_init


def _stack_halves(qh):
    lane = lax.broadcasted_iota(jnp.int32, qh.shape, 1)
    qf = qh.astype(F32)
    first = jnp.where(lane < D_QK, qf, 0.0)
    second = jnp.where(lane >= D_QK, qf, 0.0)
    return jnp.concatenate([first, second], axis=0).astype(BF16)


def _attn_prompt_kernel(lamv_ref, gaint_ref, q_ref, k_ref, vt_ref, bias_ref, o_ref,
                        qz_ref, acc_ref, m_even, m_odd, s_even, s_odd, *, t, n_bias, lam_init):
    qi = pl.program_id(0)
    bb = pl.program_id(1)
    nb = pl.num_programs(1) - 1
    n_tiles = qi + 1
    has1 = bb < nb
    has2 = bb >= 1

    def pass1_head(s_ref, m_ref, h, kj, width):
        off = pl.multiple_of(kj * t, t)
        s = lax.dot_general(k_ref[pl.ds(off, width * t), h * D_V:(h + 1) * D_V], qz_ref[h], _NT,
                            preferred_element_type=F32)
        tiles = [bias_ref[h, jnp.minimum(qi - (kj + i), n_bias - 1)] for i in range(width)]
        s = s + jnp.concatenate([jnp.concatenate([b, b], axis=1) for b in tiles], axis=0)
        s_ref[h, pl.ds(off, width * t), :] = s
        m_ref[h] = jnp.maximum(m_ref[h], jnp.max(s, axis=0, keepdims=True))

    def pass2_head(s_ref, m_ref, h, kj, width):
        off = pl.multiple_of(kj * t, t)
        p = jnp.exp2(s_ref[h, pl.ds(off, width * t), :] - m_ref[h])
        vt = jnp.concatenate([vt_ref[kj + i, h * VT_ROWS:(h + 1) * VT_ROWS, :] for i in range(width)],
                             axis=1)
        acc_ref[h] = acc_ref[h] + jnp.dot(vt, p.astype(BF16), preferred_element_type=F32)

    def over_key_tiles(per_head_steps):
        def tiles(kj, width):
            for h in range(N_HEADS):
                for step in per_head_steps:
                    step(h, kj, width)

        widest = KEY_TILES_PER_TRIP

        def trip(i, carry):
            tiles(widest * i, widest)
            return carry

        lax.fori_loop(0, n_tiles // widest, trip, 0)
        done = (n_tiles // widest) * widest
        width = widest // 2
        while width >= 1:
            @pl.when((n_tiles & width) != 0)
            def _(width=width, done=done):
                tiles(done, width)

            done = done + (n_tiles & width)
            width //= 2

    def step(s_new, m_new, s_old, m_old):
        p1 = functools.partial(pass1_head, s_new, m_new)
        p2 = functools.partial(pass2_head, s_old, m_old)

        @pl.when(has1)
        def _():
            for h in range(N_HEADS):
                qz_ref[h] = _stack_halves(q_ref[:, h * D_V:(h + 1) * D_V])
            m_new[...] = jnp.full(m_new.shape, -jnp.inf, F32)

        @pl.when(jnp.logical_and(has1, has2))
        def _():
            over_key_tiles((p1, p2))

        @pl.when(jnp.logical_not(has2))
        def _():
            over_key_tiles((p1,))

        @pl.when(jnp.logical_not(has1))
        def _():
            over_key_tiles((p2,))

    @pl.when(has2)
    def _():
        acc_ref[...] = jnp.zeros(acc_ref.shape, F32)

    @pl.when(jnp.logical_not(has2))
    def _():
        o_ref[...] = jnp.zeros(o_ref.shape, o_ref.dtype)

    @pl.when(bb % 2 == 0)
    def _():
        step(s_even, m_even, s_odd, m_odd)

    @pl.when(bb % 2 == 1)
    def _():
        step(s_odd, m_odd, s_even, m_even)

    @pl.when(has2)
    def _():
        lam = _diff_lambda(lamv_ref, lam_init)
        for h in range(N_HEADS):
            acc = acc_ref[h]
            o = acc[:D_V] / acc[D_V:D_V + 1]
            od = o[:, :t] - lam * o[:, t:]
            ms = jnp.mean(od * od, axis=0, keepdims=True)
            y = od * lax.rsqrt(ms + LN_EPS) * gaint_ref[:, h:h + 1] * (1.0 - lam_init)
            o_ref[:, h * D_V:(h + 1) * D_V] = y.T.astype(BF16)


def _attn_sample_kernel(lamv_ref, gain_ref, q_ref, kn_ref, vn_ref, kc_ref, vc_ref, bias_ref, biasn_ref,
                        o_ref, qz_ref, m_ref, acc_ref, s_ref, sn_ref, *, n_varying, lam_init):
    j = pl.program_id(1)
    nk = pl.num_programs(1) // 2
    ts = q_ref.shape[0]
    tc = kc_ref.shape[0] // N_HEADS

    @pl.when(j == 0)
    def _():
        for h in range(N_HEADS):
            qz_ref[h] = _stack_halves(q_ref[:, h * D_V:(h + 1) * D_V])
        m_ref[...] = jnp.full(m_ref.shape, -jnp.inf, F32)
        acc_ref[...] = jnp.zeros(acc_ref.shape, F32)

    @pl.when(j < nk)
    def _():
        bi = jnp.maximum(j - (nk - n_varying - 1), 0)
        for h in range(N_HEADS):
            kh = kc_ref[pl.ds(h, tc, stride=N_HEADS), :].astype(BF16)
            s = lax.dot_general(qz_ref[h], kh, _NT, preferred_element_type=F32)
            bias = bias_ref[h, bi]
            s = s + jnp.concatenate([bias, bias], axis=0)
            s_ref[h, j] = s
            m_ref[h] = jnp.maximum(m_ref[h], jnp.max(s, axis=-1, keepdims=True))

    @pl.when(j == nk - 1)
    def _():
        for h in range(N_HEADS):
            s = lax.dot_general(qz_ref[h], kn_ref[:, h * D_V:(h + 1) * D_V], _NT,
                                preferred_element_type=F32)
            bias = biasn_ref[h]
            s = s + jnp.concatenate([bias, bias], axis=0)
            sn_ref[h] = s
            m_ref[h] = jnp.maximum(m_ref[h], jnp.max(s, axis=-1, keepdims=True))

    def weighted_values(p, v):
        v1 = jnp.concatenate([v, jnp.ones(v.shape, BF16)], axis=1)
        return jnp.dot(p.astype(BF16), v1, preferred_element_type=F32)

    @pl.when(j >= nk)
    def _():
        for h in range(N_HEADS):
            p = jnp.exp2(s_ref[h, j - nk] - m_ref[h])
            vh = vc_ref[pl.ds(h, tc, stride=N_HEADS), :].astype(BF16)
            acc_ref[h] = acc_ref[h] + weighted_values(p, vh)

    @pl.when(j == 2 * nk - 1)
    def _():
        lam = _diff_lambda(lamv_ref, lam_init)
        for h in range(N_HEADS):
            hs = slice(h * D_V, (h + 1) * D_V)
            p = jnp.exp2(sn_ref[h] - m_ref[h])
            acc = acc_ref[h] + weighted_values(p, vn_ref[:, hs])
            o = acc[:, :D_V] / acc[:, D_V:D_V + 1]
            od = o[:ts] - lam * o[ts:]
            ms = jnp.mean(od * od, axis=-1, keepdims=True)
            y = od * lax.rsqrt(ms + LN_EPS) * gain_ref[h:h + 1, :] * (1.0 - lam_init)
            o_ref[:, hs] = y.astype(BF16)


def _out_proj_ln(x_ref, a_ref, gm_ref, wo_ref, g1_ref, b1_ref, alpha):
    mixed = jnp.concatenate([a_ref[...], gm_ref[...]], axis=1)
    r = jnp.dot(mixed, wo_ref[...], preferred_element_type=F32)
    return _layer_norm(alpha * x_ref[...] + r, g1_ref[...], b1_ref[...])


def _ffn_up(x1b, wu_ref, c):
    hc = jnp.dot(x1b, wu_ref[:, c * FFN_CHUNK:(c + 1) * FFN_CHUNK], preferred_element_type=F32)
    return jnp.square(jnp.maximum(hc, 0.0)).astype(BF16)


def _ffn_down(hc, wd_ref, c):
    return jnp.dot(hc, wd_ref[c * FFN_CHUNK:(c + 1) * FFN_CHUNK, :], preferred_element_type=F32)


def _finish_kernel(x_ref, a_ref, gm_ref, wo_ref, g1_ref, b1_ref, wu_ref, wd_ref, g2_ref, b2_ref, y_ref,
                   *, alpha):
    tm = x_ref.shape[0]
    bounds = [0] + [tm * f // 4 for f in FINISH_PART_QUARTERS] + [tm]
    parts = [slice(lo, hi) for lo, hi in zip(bounds[:-1], bounds[1:])]

    def out_proj_ln(rs):
        mixed = jnp.concatenate([a_ref[rs, :], gm_ref[rs, :]], axis=1)
        r = jnp.dot(mixed, wo_ref[...], preferred_element_type=F32)
        return _layer_norm(alpha * x_ref[rs, :] + r, g1_ref[...], b1_ref[...])

    def ffn(x1):
        x1b = x1.astype(BF16)
        f = alpha * x1
        for c in range(wu_ref.shape[1] // FFN_CHUNK):
            f = f + _ffn_down(_ffn_up(x1b, wu_ref, c), wd_ref, c)
        return f

    x1 = out_proj_ln(parts[0])
    for j, rs in enumerate(parts):
        x1_next = out_proj_ln(parts[j + 1]) if j + 1 < len(parts) else None
        y_ref[rs, :] = _layer_norm(ffn(x1), g2_ref[...], b2_ref[...])
        x1 = x1_next


def _const_spec(shape):
    nd = len(shape)
    return pl.BlockSpec(shape, lambda *_: (0,) * nd, pipeline_mode=pl.Buffered(1))


def _params(*sem):
    return pltpu.CompilerParams(dimension_semantics=sem, vmem_limit_bytes=VMEM_LIMIT)


def _project(x2d, w_in, lng, lnb, w_spatial, bsp, *, chunk, att_tile, sample):
    n, d = x2d.shape
    tm = min(ROW_TILE, n)
    row = lambda width: pl.BlockSpec((tm, width), lambda i: (i, 0))
    per_head = pl.BlockSpec((tm * N_HEADS, D_V), lambda i: (i, 0))
    out_shape = [jax.ShapeDtypeStruct((n, D_ATT), BF16), jax.ShapeDtypeStruct((n * N_HEADS, D_V), F32),
                 jax.ShapeDtypeStruct((n * N_HEADS, D_V), F32), jax.ShapeDtypeStruct((n, D_ATT), BF16)]
    out_specs = [row(D_ATT), per_head, per_head, row(D_ATT)]
    if sample:
        out_shape += [jax.ShapeDtypeStruct((n, D_ATT), BF16), jax.ShapeDtypeStruct((n, D_GMLP), BF16),
                      jax.ShapeDtypeStruct((n * N_GROUPS, GROUP_DIM), F32)]
        out_specs += [row(D_ATT), row(D_GMLP), pl.BlockSpec((tm * N_GROUPS, GROUP_DIM), lambda i: (i, 0))]
    else:
        out_shape += [jax.ShapeDtypeStruct((n // att_tile, N_HEADS * VT_ROWS, att_tile), BF16),
                      jax.ShapeDtypeStruct((n, D_GMLP), BF16)]
        out_specs += [pl.BlockSpec((tm // att_tile, N_HEADS * VT_ROWS, att_tile), lambda i: (i, 0, 0)), row(D_GMLP)]
    return pl.pallas_call(
        functools.partial(_proj_kernel, chunk=chunk, att_tile=att_tile, sample=sample),
        grid=(n // tm,),
        in_specs=[row(d), _const_spec(w_in.shape), _const_spec(lng.shape), _const_spec(lnb.shape),
                  _const_spec(w_spatial.shape), _const_spec(bsp.shape)],
        out_specs=out_specs,
        out_shape=out_shape,
        compiler_params=_params("parallel"),
        name="proj_sample" if sample else "proj_prompt",
    )(x2d, w_in, lng, lnb, w_spatial, bsp)


def _finish(x2d, attn, gm, w_out, g1, b1, w_up, w_down, g2, b2, *, alpha, name):
    n, d = x2d.shape
    tm = min(FINISH_ROW_TILE, n)
    row = lambda width: pl.BlockSpec((tm, width), lambda i: (i, 0))
    return pl.pallas_call(
        functools.partial(_finish_kernel, alpha=alpha),
        grid=(n // tm,),
        in_specs=[row(d), row(D_ATT), row(D_GMLP), _const_spec(w_out.shape), _const_spec(g1.shape),
                  _const_spec(b1.shape), _const_spec(w_up.shape), _const_spec(w_down.shape),
                  _const_spec(g2.shape), _const_spec(b2.shape)],
        out_specs=row(d),
        out_shape=jax.ShapeDtypeStruct((n, d), F32),
        compiler_params=_params("parallel"),
        name=name,
    )(x2d, attn, gm, w_out, g1, b1, w_up, w_down, g2, b2)


def kernel(x_prompt, x_sample, cache_k, cache_v, rel_bias_table, w_in, lambda_q1, lambda_k1, lambda_q2,
           lambda_k2, subln_gain, gmlp_ln_gain, gmlp_ln_bias, w_spatial, b_spatial, w_out, ln1_gain,
           ln1_bias, w_ffn_up, w_ffn_down, ln2_gain, ln2_bias):
    depth = w_in.shape[0]
    assert depth == 1, "single-layer step"
    b_p, t_p, d_model = x_prompt.shape
    b_s, t_s, _ = x_sample.shape
    past = cache_k.shape[2]
    alpha = (2.0 * depth) ** 0.25
    lam_init = _lambda_init(0)
    t = ATT_TILE
    tc = CACHE_TILE
    assert t_p % t == 0 and past % tc == 0 and t % CHUNK == 0 and t_s == CHUNK and past % GMLP_CHUNK == 0
    assert ROW_TILE % t == 0 and (b_p * t_p) % ROW_TILE == 0 and (b_s * t_s) % ROW_TILE == 0

    bucket_table = _bucket_of_distance(past + t_s)
    n_prompt_bias = _num_varying_tiles(t, bucket_table) + 1
    n_sample_var = _num_varying_tiles(tc, bucket_table) - 1
    n_sample_bias = n_sample_var + 1
    assert past // tc > n_sample_var

    bias_p, bias_s, bias_n = pl.pallas_call(
        functools.partial(_bias_kernel, n_prompt=n_prompt_bias, t=t, n_sample=n_sample_bias, tc=tc,
                          past=past, ts=t_s, bucket_table=bucket_table),
        in_specs=[pl.BlockSpec(memory_space=pltpu.SMEM)],
        out_shape=[jax.ShapeDtypeStruct((N_HEADS, n_prompt_bias, t, t), F32),
                   jax.ShapeDtypeStruct((N_HEADS, n_sample_bias, t_s, tc), F32),
                   jax.ShapeDtypeStruct((N_HEADS, t_s, t_s), F32)],
        compiler_params=pltpu.CompilerParams(vmem_limit_bytes=VMEM_LIMIT),
        name="rel_bias_tiles",
    )(rel_bias_table)

    w_in_b = w_in[0].astype(BF16)
    w_out_b = w_out[0].astype(BF16)
    w_up_b = w_ffn_up[0].astype(BF16)
    w_down_b = w_ffn_down[0].astype(BF16)
    lng = gmlp_ln_gain[0].reshape(1, D_GMLP)
    lnb = gmlp_ln_bias[0].reshape(1, D_GMLP)
    bsp = jnp.repeat(b_spatial[0].T, GROUP_DIM, axis=1)
    g1, b1 = ln1_gain[0].reshape(1, d_model), ln1_bias[0].reshape(1, d_model)
    g2, b2 = ln2_gain[0].reshape(1, d_model), ln2_bias[0].reshape(1, d_model)
    lamv = jnp.stack([lambda_q1[0], lambda_k1[0], lambda_q2[0], lambda_k2[0]], axis=0)
    gain = subln_gain[0]

    xp = x_prompt.reshape(b_p * t_p, d_model)
    q_p, k_p, v_p, kb_p, vt_p, gm_p = _project(xp, w_in_b, lng, lnb, w_spatial[0], bsp,
                                               chunk=GMLP_CHUNK, att_tile=t, sample=False)
    nq = t_p // t
    gain_t = gain.T
    attn_p = pl.pallas_call(
        functools.partial(_attn_prompt_kernel, t=t, n_bias=n_prompt_bias, lam_init=lam_init),
        grid=(nq, b_p + 1),
        in_specs=[_const_spec(lamv.shape), _const_spec(gain_t.shape),
                  pl.BlockSpec((t, D_ATT), lambda i, bb: (jnp.minimum(bb, b_p - 1) * nq + i, 0)),
                  pl.BlockSpec((t_p, D_ATT), lambda i, bb: (jnp.minimum(bb, b_p - 1), 0)),
                  pl.BlockSpec((nq, N_HEADS * VT_ROWS, t), lambda i, bb: (jnp.maximum(bb - 1, 0), 0, 0)),
                  _const_spec(bias_p.shape)],
        out_specs=pl.BlockSpec((t, D_ATT), lambda i, bb: (jnp.maximum(bb - 1, 0) * nq + i, 0)),
        out_shape=jax.ShapeDtypeStruct((b_p * t_p, D_ATT), BF16),
        scratch_shapes=[pltpu.VMEM((N_HEADS, 2 * t, D_V), BF16), pltpu.VMEM((N_HEADS, VT_ROWS, 2 * t), F32),
                        pltpu.VMEM((N_HEADS, 1, 2 * t), F32), pltpu.VMEM((N_HEADS, 1, 2 * t), F32),
                        pltpu.VMEM((N_HEADS, t_p, 2 * t), F32), pltpu.VMEM((N_HEADS, t_p, 2 * t), F32)],
        compiler_params=_params("arbitrary", "arbitrary"),
        name="attn_prompt",
    )(lamv, gain_t, q_p, kb_p, vt_p, bias_p)
    y_p = _finish(xp, attn_p, gm_p, w_out_b, g1, b1, w_up_b, w_down_b, g2, b2, alpha=alpha,
                  name="finish_prompt")

    xs = x_sample.reshape(b_s * t_s, d_model)
    q_s, k_s, v_s, kb_s, vb_s, gm_s, gn_s = _project(xs, w_in_b, lng, lnb, w_spatial[0], bsp[:t_s],
                                                     chunk=t_s, att_tile=t, sample=True)
    nc = past // tc
    ck = cache_k.reshape(b_s, past * N_HEADS, D_V)
    cv = cache_v.reshape(b_s, past * N_HEADS, D_V)
    key_spec = pl.BlockSpec((None, tc * N_HEADS, D_V), lambda b, j: (b, jnp.minimum(j, nc - 1), 0))
    val_spec = pl.BlockSpec((None, tc * N_HEADS, D_V), lambda b, j: (b, jnp.maximum(j - nc, 0), 0))
    new_spec = pl.BlockSpec((t_s, D_ATT), lambda b, j: (b, 0))
    attn_s = pl.pallas_call(
        functools.partial(_attn_sample_kernel, n_varying=n_sample_var, lam_init=lam_init),
        grid=(b_s, 2 * nc),
        in_specs=[_const_spec(lamv.shape), _const_spec(gain.shape), new_spec, new_spec, new_spec,
                  key_spec, val_spec, _const_spec(bias_s.shape), _const_spec(bias_n.shape)],
        out_specs=new_spec,
        out_shape=jax.ShapeDtypeStruct((b_s * t_s, D_ATT), BF16),
        scratch_shapes=[pltpu.VMEM((N_HEADS, 2 * t_s, D_V), BF16), pltpu.VMEM((N_HEADS, 2 * t_s, 1), F32),
                        pltpu.VMEM((N_HEADS, 2 * t_s, 2 * D_V), F32),
                        pltpu.VMEM((N_HEADS, nc, 2 * t_s, tc), F32),
                        pltpu.VMEM((N_HEADS, 2 * t_s, t_s), F32)],
        compiler_params=_params("parallel", "arbitrary"),
        name="attn_sample",
    )(lamv, gain, q_s, kb_s, vb_s, ck, cv, bias_s, bias_n)
    y_s = _finish(xs, attn_s, gm_s, w_out_b, g1, b1, w_up_b, w_down_b, g2, b2, alpha=alpha,
                  name="finish_sample")

    head_shape_p = (depth, b_p, t_p, N_HEADS, D_V)
    head_shape_s = (depth, b_s, t_s, N_HEADS, D_V)
    return (y_p.reshape(b_p, t_p, d_model), y_s.reshape(b_s, t_s, d_model),
            k_p.reshape(head_shape_p), v_p.reshape(head_shape_p),
            k_s.reshape(head_shape_s), v_s.reshape(head_shape_s),
            gn_s.reshape(depth, b_s, t_s, N_GROUPS, GROUP_DIM))
```

```python
import functools
import math

import numpy as np
import jax
import jax.numpy as jnp
from jax import lax
from jax.experimental import pallas as pl
from jax.experimental.pallas import tpu as pltpu

F32 = jnp.float32
BF16 = jnp.bfloat16

CHUNK = 64
N_HEADS = 4
D_QK = 64
D_V = 2 * D_QK
D_ATT = N_HEADS * D_V
N_GROUPS = 4
GROUP_DIM = 128
D_GMLP = N_GROUPS * GROUP_DIM
GMLP_CHUNK = 128
NUM_BUCKETS = 32
MAX_DISTANCE = 1024
LN_EPS = 1e-5
NEG_INF = -1e30
QK_SCALE = D_QK ** -0.5
LOG2E = math.log2(math.e)
ONES_ROWS = 16
VT_ROWS = D_V + ONES_ROWS

ROW_TILE = 1024
FINISH_ROW_TILE = 512
FINISH_PART_QUARTERS = (2,)
ATT_TILE = 256
KEY_TILES_PER_TRIP = 2
CACHE_TILE = 2048
FFN_CHUNK = 1024
VMEM_LIMIT = 56 * 1024 * 1024
FUSED_VMEM_LIMIT = 60 * 1024 * 1024

_NT = (((1,), (1,)), ((), ()))


def _lambda_init(layer):
    return 0.8 - 0.6 * math.exp(-0.3 * layer)


def _bucket_of_distance(max_n):
    half = NUM_BUCKETS // 2
    max_exact = half // 2
    n = np.arange(max_n + 1)

    def table(dtype):
        nf = np.maximum(n, 1).astype(dtype)
        scaled = np.log(nf / dtype(max_exact)) / dtype(math.log(MAX_DISTANCE / max_exact)) * dtype(half - max_exact)
        large = np.minimum(max_exact + scaled.astype(np.int64), half - 1)
        return np.where(n < max_exact, n, large)

    t64 = table(np.float64)
    assert np.array_equal(t64, table(np.float32)), "bucket boundaries are precision sensitive"
    return t64


def _rel_bucket_static(rel, bucket_table):
    return (NUM_BUCKETS // 2 if rel > 0 else 0) + int(bucket_table[abs(rel)])


def _bias_tile(tab_ref, head, d0, nrows, ncols, bucket_table, *, keys_on_rows=False):
    row = lax.broadcasted_iota(jnp.int32, (nrows, ncols), 0)
    col = lax.broadcasted_iota(jnp.int32, (nrows, ncols), 1)
    if keys_on_rows:
        d = (row - col) + d0
        dmin, dmax = d0 - (ncols - 1), d0 + nrows - 1
    else:
        d = (col - row) + d0
        dmin, dmax = d0 - (nrows - 1), d0 + ncols - 1
    prev = _rel_bucket_static(dmin, bucket_table)
    val = jnp.full((nrows, ncols), tab_ref[prev, head] * LOG2E, F32)
    for dd in range(dmin + 1, dmax + 1):
        b = _rel_bucket_static(dd, bucket_table)
        if b != prev:
            val = jnp.where(d >= dd, tab_ref[b, head] * LOG2E, val)
            prev = b
    return val


def _bias_kernel(tab_ref, bp_ref, bs_ref, bn_ref, *, n_prompt, t, n_sample, tc, past, ts, bucket_table):
    key = lax.broadcasted_iota(jnp.int32, (t, t), 0)
    qry = lax.broadcasted_iota(jnp.int32, (t, t), 1)
    visible = (key // CHUNK) <= (qry // CHUNK)
    n_cache_tiles = past // tc
    for h in range(N_HEADS):
        for dl in range(n_prompt):
            tile = _bias_tile(tab_ref, h, -dl * t, t, t, bucket_table, keys_on_rows=True)
            if dl == 0:
                tile = jnp.where(visible, tile, NEG_INF)
            bp_ref[h, dl] = tile
        for i in range(n_sample):
            j = n_cache_tiles - n_sample + i
            bs_ref[h, i] = _bias_tile(tab_ref, h, j * tc - past, ts, tc, bucket_table)
        bn_ref[h] = _bias_tile(tab_ref, h, 0, ts, ts, bucket_table)


def _far_distance(bucket_table):
    n = len(bucket_table) - 1
    while n > 0 and bucket_table[n - 1] == bucket_table[-1]:
        n -= 1
    return n


def _num_varying_tiles(tile, bucket_table):
    far = _far_distance(bucket_table)
    n = 1
    while (n - 1) * tile + 1 < far:
        n += 1
    return n


def _gelu_tanh(x):
    c = math.sqrt(2.0 / math.pi)
    return 0.5 * x * (1.0 + jnp.tanh(c * (x + 0.044715 * (x * x * x))))


def _layer_norm(x, g, b):
    mu = jnp.mean(x, axis=-1, keepdims=True)
    xc = x - mu
    var = jnp.mean(xc * xc, axis=-1, keepdims=True)
    return xc * lax.rsqrt(var + LN_EPS) * g + b


def _store_heads(ref, x, n):
    rows = x.shape[0]
    for h in range(n):
        ref[pl.ds(h, rows, stride=n), :] = x[:, h * 128:(h + 1) * 128]


def _proj_kernel(x_ref, w_ref, lng_ref, lnb_ref, ws_ref, bsp_ref, *out_refs, chunk, att_tile, sample):
    if sample:
        q_ref, kf_ref, vf_ref, kb_ref, vb_ref, gm_ref, gn_ref = out_refs
    else:
        q_ref, kf_ref, vf_ref, kb_ref, vt_ref, gm_ref = out_refs
    tm = x_ref.shape[0]
    xb = x_ref[...].astype(BF16)

    def seg(lo, hi):
        return jnp.dot(xb, w_ref[:, lo:hi], preferred_element_type=F32)

    g_lin = seg(3 * D_ATT + D_GMLP, 3 * D_ATT + 2 * D_GMLP)
    u_lin = seg(3 * D_ATT, 3 * D_ATT + D_GMLP)

    lng = lng_ref[...]
    lnb = lnb_ref[...]
    g = _gelu_tanh(g_lin)
    gn = jnp.concatenate(
        [_layer_norm(g[:, i * GROUP_DIM:(i + 1) * GROUP_DIM],
                     lng[:, i * GROUP_DIM:(i + 1) * GROUP_DIM],
                     lnb[:, i * GROUP_DIM:(i + 1) * GROUP_DIM]) for i in range(N_GROUPS)], axis=1)
    if sample:
        _store_heads(gn_ref, gn, N_GROUPS)
    gnb = gn.astype(BF16)

    v = seg(2 * D_ATT, 3 * D_ATT)
    _store_heads(vf_ref, v, N_HEADS)
    if sample:
        vb_ref[...] = v.astype(BF16)
    else:
        ones = jnp.ones((ONES_ROWS, att_tile), F32)
        for j in range(tm // att_tile):
            vt = v[j * att_tile:(j + 1) * att_tile, :].T
            vt_ref[j] = jnp.concatenate(
                [part for h in range(N_HEADS) for part in (vt[h * D_V:(h + 1) * D_V], ones)],
                axis=0).astype(BF16)

    u = _gelu_tanh(u_lin)

    k = seg(D_ATT, 2 * D_ATT)
    _store_heads(kf_ref, k, N_HEADS)
    kb_ref[...] = k.astype(BF16)
    q_ref[...] = (seg(0, D_ATT) * (QK_SCALE * LOG2E)).astype(BF16)

    ri = lax.broadcasted_iota(jnp.int32, (chunk, chunk), 0)
    ci = lax.broadcasted_iota(jnp.int32, (chunk, chunk), 1)
    keep = (ri // CHUNK) >= (ci // CHUNK)
    ws = [jnp.where(keep, ws_ref[i, :chunk, :chunk], 0.0).astype(BF16) for i in range(N_GROUPS)]
    bsp = bsp_ref[...]
    for c in range(tm // chunk):
        r0 = c * chunk
        sp = jnp.concatenate(
            [jnp.dot(ws[i], gnb[r0:r0 + chunk, i * GROUP_DIM:(i + 1) * GROUP_DIM],
                     preferred_element_type=F32) for i in range(N_GROUPS)], axis=1)
        gm_ref[r0:r0 + chunk, :] = (u[r0:r0 + chunk, :] * (sp + bsp)).astype(BF16)


def _diff_lambda(lamv_ref, lam_init):
    lv = lamv_ref[...]
    s1 = jnp.sum(lv[0:1, :] * lv[1:2, :], axis=-1, keepdims=True)
    s2 = jnp.sum(lv[2:3, :] * lv[3:4, :], axis=-1, keepdims=True)
    return jnp.exp(s1) - jnp.exp(s2) + lam_init


def _stack_halves(qh):
    lane = lax.broadcasted_iota(jnp.int32, qh.shape, 1)
    qf = qh.astype(F32)
    first = jnp.where(lane < D_QK, qf, 0.0)
    second = jnp.where(lane >= D_QK, qf, 0.0)
    return jnp.concatenate([first, second], axis=0).astype(BF16)


def _attn_prompt_kernel(lamv_ref, gaint_ref, q_ref, k_ref, vt_ref, bias_ref, o_ref,
                        qz_ref, acc_ref, m_even, m_odd, s_even, s_odd, *, t, n_bias, lam_init):
    qi = pl.program_id(0)
    bb = pl.program_id(1)
    nb = pl.num_programs(1) - 1
    n_tiles = qi + 1
    has1 = bb < nb
    has2 = bb >= 1

    def pass1_head(s_ref, m_ref, h, kj, width):
        off = pl.multiple_of(kj * t, t)
        s = lax.dot_general(k_ref[pl.ds(off, width * t), h * D_V:(h + 1) * D_V], qz_ref[h], _NT,
                            preferred_element_type=F32)
        tiles = [bias_ref[h, jnp.minimum(qi - (kj + i), n_bias - 1)] for i in range(width)]
        s = s + jnp.concatenate([jnp.concatenate([b, b], axis=1) for b in tiles], axis=0)
        s_ref[h, pl.ds(off, width * t), :] = s
        m_ref[h] = jnp.maximum(m_ref[h], jnp.max(s, axis=0, keepdims=True))

    def pass2_head(s_ref, m_ref, h, kj, width):
        off = pl.multiple_of(kj * t, t)
        p = jnp.exp2(s_ref[h, pl.ds(off, width * t), :] - m_ref[h])
        vt = jnp.concatenate([vt_ref[kj + i, h * VT_ROWS:(h + 1) * VT_ROWS, :] for i in range(width)],
                             axis=1)
        acc_ref[h] = acc_ref[h] + jnp.dot(vt, p.astype(BF16), preferred_element_type=F32)

    def over_key_tiles(per_head_steps):
        def tiles(kj, width):
            for h in range(N_HEADS):
                for step in per_head_steps:
                    step(h, kj, width)

        widest = KEY_TILES_PER_TRIP

        def trip(i, carry):
            tiles(widest * i, widest)
            return carry

        lax.fori_loop(0, n_tiles // widest, trip, 0)
        done = (n_tiles // widest) * widest
        width = widest // 2
        while width >= 1:
            @pl.when((n_tiles & width) != 0)
            def _(width=width, done=done):
                tiles(done, width)

            done = done + (n_tiles & width)
            width //= 2

    def step(s_new, m_new, s_old, m_old):
        p1 = functools.partial(pass1_head, s_new, m_new)
        p2 = functools.partial(pass2_head, s_old, m_old)

        @pl.when(has1)
        def _():
            for h in range(N_HEADS):
                qz_ref[h] = _stack_halves(q_ref[:, h * D_V:(h + 1) * D_V])
            m_new[...] = jnp.full(m_new.shape, -jnp.inf, F32)

        @pl.when(jnp.logical_and(has1, has2))
        def _():
            over_key_tiles((p1, p2))

        @pl.when(jnp.logical_not(has2))
        def _():
            over_key_tiles((p1,))

        @pl.when(jnp.logical_not(has1))
        def _():
            over_key_tiles((p2,))

    @pl.when(has2)
    def _():
        acc_ref[...] = jnp.zeros(acc_ref.shape, F32)

    @pl.when(jnp.logical_not(has2))
    def _():
        o_ref[...] = jnp.zeros(o_ref.shape, o_ref.dtype)

    @pl.when(bb % 2 == 0)
    def _():
        step(s_even, m_even, s_odd, m_odd)

    @pl.when(bb % 2 == 1)
    def _():
        step(s_odd, m_odd, s_even, m_even)

    @pl.when(has2)
    def _():
        lam = _diff_lambda(lamv_ref, lam_init)
        for h in range(N_HEADS):
            acc = acc_ref[h]
            o = acc[:D_V] / acc[D_V:D_V + 1]
            od = o[:, :t] - lam * o[:, t:]
            ms = jnp.mean(od * od, axis=0, keepdims=True)
            y = od * lax.rsqrt(ms + LN_EPS) * gaint_ref[:, h:h + 1] * (1.0 - lam_init)
            o_ref[:, h * D_V:(h + 1) * D_V] = y.T.astype(BF16)


def _sample_attention_chunk(chunk, n_chunks, lamv_ref, gain_ref, q_ref, kn_ref, vn_ref, kc_ref, vc_ref,
                            bias_ref, biasn_ref, o_ref, qz_ref, m_ref, acc_ref, s_ref, *, n_varying, lam_init):
    ts = q_ref.shape[0]
    tc = kc_ref.shape[0] // N_HEADS

    @pl.when(chunk == 0)
    def _():
        for h in range(N_HEADS):
            qz_ref[h] = _stack_halves(q_ref[:, h * D_V:(h + 1) * D_V])
        m_ref[...] = jnp.full(m_ref.shape, -jnp.inf, F32)
        acc_ref[...] = jnp.zeros(acc_ref.shape, F32)

    def raise_max(h, s):
        m_old = m_ref[h]
        m_new = jnp.maximum(m_old, jnp.max(s, axis=-1, keepdims=True))
        m_ref[h] = m_new
        acc_ref[h] = acc_ref[h] * jnp.exp2(m_old - m_new)
        return m_new

    def weighted_values(p, v):
        v1 = jnp.concatenate([v, jnp.ones(v.shape, BF16)], axis=1)
        return jnp.dot(p.astype(BF16), v1, preferred_element_type=F32)

    bi = jnp.maximum(chunk - (n_chunks - n_varying - 1), 0)
    for h in range(N_HEADS):
        kh = kc_ref[pl.ds(h, tc, stride=N_HEADS), :].astype(BF16)
        s = lax.dot_general(qz_ref[h], kh, _NT, preferred_element_type=F32)
        bias = bias_ref[h, bi]
        s = s + jnp.concatenate([bias, bias], axis=0)
        s_ref[h] = s
        raise_max(h, s)
    for h in range(N_HEADS):
        p = jnp.exp2(s_ref[h] - m_ref[h])
        vh = vc_ref[pl.ds(h, tc, stride=N_HEADS), :].astype(BF16)
        acc_ref[h] = acc_ref[h] + weighted_values(p, vh)

    @pl.when(chunk == n_chunks - 1)
    def _():
        lam = _diff_lambda(lamv_ref, lam_init)
        for h in range(N_HEADS):
            hs = slice(h * D_V, (h + 1) * D_V)
            s = lax.dot_general(qz_ref[h], kn_ref[:, hs], _NT, preferred_element_type=F32)
            bias = biasn_ref[h]
            s = s + jnp.concatenate([bias, bias], axis=0)
            m_new = raise_max(h, s)
            acc = acc_ref[h] + weighted_values(jnp.exp2(s - m_new), vn_ref[:, hs])
            o = acc[:, :D_V] / acc[:, D_V:D_V + 1]
            od = o[:ts] - lam * o[ts:]
            ms = jnp.mean(od * od, axis=-1, keepdims=True)
            y = od * lax.rsqrt(ms + LN_EPS) * gain_ref[h:h + 1, :] * (1.0 - lam_init)
            o_ref[:, hs] = y.astype(BF16)


_N_SAMPLE_INPUTS = 9


def _ffn_up(x1b, wu_ref, c):
    hc = jnp.dot(x1b, wu_ref[:, c * FFN_CHUNK:(c + 1) * FFN_CHUNK], preferred_element_type=F32)
    return jnp.square(jnp.maximum(hc, 0.0)).astype(BF16)


def _ffn_down(hc, wd_ref, c):
    return jnp.dot(hc, wd_ref[c * FFN_CHUNK:(c + 1) * FFN_CHUNK, :], preferred_element_type=F32)


def _finish_kernel(x_ref, a_ref, gm_ref, wo_ref, g1_ref, b1_ref, wu_ref, wd_ref, g2_ref, b2_ref, *rest,
                   alpha, sample_cfg):
    if sample_cfg is None:
        (y_ref,) = rest
    else:
        y_ref = rest[_N_SAMPLE_INPUTS]
        i = pl.program_id(0)
        per_batch = sample_cfg["chunks_per_batch"]
        _sample_attention_chunk(i % per_batch, per_batch, *rest[:_N_SAMPLE_INPUTS], *rest[_N_SAMPLE_INPUTS + 1:],
                                n_varying=sample_cfg["n_varying"], lam_init=sample_cfg["lam_init"])

    tm = x_ref.shape[0]
    bounds = [0] + [tm * f // 4 for f in FINISH_PART_QUARTERS] + [tm]
    parts = [slice(lo, hi) for lo, hi in zip(bounds[:-1], bounds[1:])]

    def out_proj_ln(rs):
        mixed = jnp.concatenate([a_ref[rs, :], gm_ref[rs, :]], axis=1)
        r = jnp.dot(mixed, wo_ref[...], preferred_element_type=F32)
        return _layer_norm(alpha * x_ref[rs, :] + r, g1_ref[...], b1_ref[...])

    def ffn(x1):
        x1b = x1.astype(BF16)
        f = alpha * x1
        for c in range(wu_ref.shape[1] // FFN_CHUNK):
            f = f + _ffn_down(_ffn_up(x1b, wu_ref, c), wd_ref, c)
        return f

    x1 = out_proj_ln(parts[0])
    for j, rs in enumerate(parts):
        x1_next = out_proj_ln(parts[j + 1]) if j + 1 < len(parts) else None
        y_ref[rs, :] = _layer_norm(ffn(x1), g2_ref[...], b2_ref[...])
        x1 = x1_next


def _const_spec(shape):
    nd = len(shape)
    return pl.BlockSpec(shape, lambda *_: (0,) * nd, pipeline_mode=pl.Buffered(1))


def _params(*sem):
    return pltpu.CompilerParams(dimension_semantics=sem, vmem_limit_bytes=VMEM_LIMIT)


def _project(x2d, w_in, lng, lnb, w_spatial, bsp, *, chunk, att_tile, sample):
    n, d = x2d.shape
    tm = min(ROW_TILE, n)
    row = lambda width: pl.BlockSpec((tm, width), lambda i: (i, 0))
    per_head = pl.BlockSpec((tm * N_HEADS, D_V), lambda i: (i, 0))
    out_shape = [jax.ShapeDtypeStruct((n, D_ATT), BF16), jax.ShapeDtypeStruct((n * N_HEADS, D_V), F32),
                 jax.ShapeDtypeStruct((n * N_HEADS, D_V), F32), jax.ShapeDtypeStruct((n, D_ATT), BF16)]
    out_specs = [row(D_ATT), per_head, per_head, row(D_ATT)]
    if sample:
        out_shape += [jax.ShapeDtypeStruct((n, D_ATT), BF16), jax.ShapeDtypeStruct((n, D_GMLP), BF16),
                      jax.ShapeDtypeStruct((n * N_GROUPS, GROUP_DIM), F32)]
        out_specs += [row(D_ATT), row(D_GMLP), pl.BlockSpec((tm * N_GROUPS, GROUP_DIM), lambda i: (i, 0))]
    else:
        out_shape += [jax.ShapeDtypeStruct((n // att_tile, N_HEADS * VT_ROWS, att_tile), BF16),
                      jax.ShapeDtypeStruct((n, D_GMLP), BF16)]
        out_specs += [pl.BlockSpec((tm // att_tile, N_HEADS * VT_ROWS, att_tile), lambda i: (i, 0, 0)), row(D_GMLP)]
    return pl.pallas_call(
        functools.partial(_proj_kernel, chunk=chunk, att_tile=att_tile, sample=sample),
        grid=(n // tm,),
        in_specs=[row(d), _const_spec(w_in.shape), _const_spec(lng.shape), _const_spec(lnb.shape),
                  _const_spec(w_spatial.shape), _const_spec(bsp.shape)],
        out_specs=out_specs,
        out_shape=out_shape,
        compiler_params=_params("parallel"),
        name="proj_sample" if sample else "proj_prompt",
    )(x2d, w_in, lng, lnb, w_spatial, bsp)


def _finish(x2d, attn, gm, w_out, g1, b1, w_up, w_down, g2, b2, *, alpha, name, sample=None):
    n, d = x2d.shape
    tm = min(FINISH_ROW_TILE, n)
    nt = n // tm
    row = lambda width: pl.BlockSpec((tm, width), lambda i: (i, 0))
    in_specs = [row(d), row(D_ATT), row(D_GMLP), _const_spec(w_out.shape), _const_spec(g1.shape),
                _const_spec(b1.shape), _const_spec(w_up.shape), _const_spec(w_down.shape),
                _const_spec(g2.shape), _const_spec(b2.shape)]
    operands = [x2d, attn, gm, w_out, g1, b1, w_up, w_down, g2, b2]
    out_specs = [row(d)]
    out_shape = [jax.ShapeDtypeStruct((n, d), F32)]
    scratch = []
    sample_cfg = None
    if sample is not None:
        ck, cv = sample["cache_k"], sample["cache_v"]
        b_s, tc = ck.shape[0], sample["cache_tile"]
        per_batch = ck.shape[1] // (tc * N_HEADS)
        assert nt == b_s * per_batch, "one cache chunk per grid step"
        t_s = sample["q"].shape[0] // b_s
        new_spec = pl.BlockSpec((t_s, D_ATT), lambda i: (i // per_batch, 0))
        cache_spec = pl.BlockSpec((None, tc * N_HEADS, D_V), lambda i: (i // per_batch, i % per_batch, 0))
        small = [sample["lamv"], sample["gain"]]
        tiles = [sample["bias_cache"], sample["bias_new"]]
        in_specs += ([_const_spec(a.shape) for a in small] + [new_spec] * 3 + [cache_spec] * 2
                     + [_const_spec(a.shape) for a in tiles])
        operands += small + [sample["q"], sample["k_new"], sample["v_new"], ck, cv] + tiles
        out_specs.append(new_spec)
        out_shape.append(jax.ShapeDtypeStruct((b_s * t_s, D_ATT), BF16))
        scratch = [pltpu.VMEM((N_HEADS, 2 * t_s, D_V), BF16), pltpu.VMEM((N_HEADS, 2 * t_s, 1), F32),
                   pltpu.VMEM((N_HEADS, 2 * t_s, 2 * D_V), F32), pltpu.VMEM((N_HEADS, 2 * t_s, tc), F32)]
        sample_cfg = dict(chunks_per_batch=per_batch, n_varying=sample["n_varying"],
                          lam_init=sample["lam_init"])
    outs = pl.pallas_call(
        functools.partial(_finish_kernel, alpha=alpha, sample_cfg=sample_cfg),
        grid=(nt,),
        in_specs=in_specs,
        out_specs=out_specs,
        out_shape=out_shape,
        scratch_shapes=scratch,
        compiler_params=pltpu.CompilerParams(
            dimension_semantics=("arbitrary",),
            vmem_limit_bytes=FUSED_VMEM_LIMIT if sample is not None else VMEM_LIMIT),
        name=name,
    )(*operands)
    return outs if sample is not None else outs[0]


def kernel(x_prompt, x_sample, cache_k, cache_v, rel_bias_table, w_in, lambda_q1, lambda_k1, lambda_q2,
           lambda_k2, subln_gain, gmlp_ln_gain, gmlp_ln_bias, w_spatial, b_spatial, w_out, ln1_gain,
           ln1_bias, w_ffn_up, w_ffn_down, ln2_gain, ln2_bias):
    depth = w_in.shape[0]
    assert depth == 1, "single-layer step"
    b_p, t_p, d_model = x_prompt.shape
    b_s, t_s, _ = x_sample.shape
    past = cache_k.shape[2]
    alpha = (2.0 * depth) ** 0.25
    lam_init = _lambda_init(0)
    t = ATT_TILE
    tc = CACHE_TILE
    assert t_p % t == 0 and past % tc == 0 and t % CHUNK == 0 and t_s == CHUNK and past % GMLP_CHUNK == 0
    assert ROW_TILE % t == 0 and (b_p * t_p) % ROW_TILE == 0 and (b_s * t_s) % ROW_TILE == 0

    bucket_table = _bucket_of_distance(past + t_s)
    n_prompt_bias = _num_varying_tiles(t, bucket_table) + 1
    n_sample_var = _num_varying_tiles(tc, bucket_table) - 1
    n_sample_bias = n_sample_var + 1
    assert past // tc > n_sample_var

    bias_p, bias_s, bias_n = pl.pallas_call(
        functools.partial(_bias_kernel, n_prompt=n_prompt_bias, t=t, n_sample=n_sample_bias, tc=tc,
                          past=past, ts=t_s, bucket_table=bucket_table),
        in_specs=[pl.BlockSpec(memory_space=pltpu.SMEM)],
        out_shape=[jax.ShapeDtypeStruct((N_HEADS, n_prompt_bias, t, t), F32),
                   jax.ShapeDtypeStruct((N_HEADS, n_sample_bias, t_s, tc), F32),
                   jax.ShapeDtypeStruct((N_HEADS, t_s, t_s), F32)],
        compiler_params=pltpu.CompilerParams(vmem_limit_bytes=VMEM_LIMIT),
        name="rel_bias_tiles",
    )(rel_bias_table)

    w_in_b = w_in[0].astype(BF16)
    w_out_b = w_out[0].astype(BF16)
    w_up_b = w_ffn_up[0].astype(BF16)
    w_down_b = w_ffn_down[0].astype(BF16)
    lng = gmlp_ln_gain[0].reshape(1, D_GMLP)
    lnb = gmlp_ln_bias[0].reshape(1, D_GMLP)
    bsp = jnp.repeat(b_spatial[0].T, GROUP_DIM, axis=1)
    g1, b1 = ln1_gain[0].reshape(1, d_model), ln1_bias[0].reshape(1, d_model)
    g2, b2 = ln2_gain[0].reshape(1, d_model), ln2_bias[0].reshape(1, d_model)
    lamv = jnp.stack([lambda_q1[0], lambda_k1[0], lambda_q2[0], lambda_k2[0]], axis=0)
    gain = subln_gain[0]

    xp = x_prompt.reshape(b_p * t_p, d_model)
    q_p, k_p, v_p, kb_p, vt_p, gm_p = _project(xp, w_in_b, lng, lnb, w_spatial[0], bsp,
                                               chunk=GMLP_CHUNK, att_tile=t, sample=False)
    nq = t_p // t
    gain_t = gain.T
    attn_p = pl.pallas_call(
        functools.partial(_attn_prompt_kernel, t=t, n_bias=n_prompt_bias, lam_init=lam_init),
        grid=(nq, b_p + 1),
        in_specs=[_const_spec(lamv.shape), _const_spec(gain_t.shape),
                  pl.BlockSpec((t, D_ATT), lambda i, bb: (jnp.minimum(bb, b_p - 1) * nq + i, 0)),
                  pl.BlockSpec((t_p, D_ATT), lambda i, bb: (jnp.minimum(bb, b_p - 1), 0)),
                  pl.BlockSpec((nq, N_HEADS * VT_ROWS, t), lambda i, bb: (jnp.maximum(bb - 1, 0), 0, 0)),
                  _const_spec(bias_p.shape)],
        out_specs=pl.BlockSpec((t, D_ATT), lambda i, bb: (jnp.maximum(bb - 1, 0) * nq + i, 0)),
        out_shape=jax.ShapeDtypeStruct((b_p * t_p, D_ATT), BF16),
        scratch_shapes=[pltpu.VMEM((N_HEADS, 2 * t, D_V), BF16), pltpu.VMEM((N_HEADS, VT_ROWS, 2 * t), F32),
                        pltpu.VMEM((N_HEADS, 1, 2 * t), F32), pltpu.VMEM((N_HEADS, 1, 2 * t), F32),
                        pltpu.VMEM((N_HEADS, t_p, 2 * t), F32), pltpu.VMEM((N_HEADS, t_p, 2 * t), F32)],
        compiler_params=_params("arbitrary", "arbitrary"),
        name="attn_prompt",
    )(lamv, gain_t, q_p, kb_p, vt_p, bias_p)

    xs = x_sample.reshape(b_s * t_s, d_model)
    q_s, k_s, v_s, kb_s, vb_s, gm_s, gn_s = _project(xs, w_in_b, lng, lnb, w_spatial[0], bsp[:t_s],
                                                     chunk=t_s, att_tile=t, sample=True)
    sample = dict(lamv=lamv, gain=gain, q=q_s, k_new=kb_s, v_new=vb_s,
                  cache_k=cache_k.reshape(b_s, past * N_HEADS, D_V),
                  cache_v=cache_v.reshape(b_s, past * N_HEADS, D_V), cache_tile=tc,
                  bias_cache=bias_s, bias_new=bias_n, n_varying=n_sample_var, lam_init=lam_init)
    y_p, attn_s = _finish(xp, attn_p, gm_p, w_out_b, g1, b1, w_up_b, w_down_b, g2, b2, alpha=alpha,
                          name="finish_prompt_attn_sample", sample=sample)
    y_s = _finish(xs, attn_s, gm_s, w_out_b, g1, b1, w_up_b, w_down_b, g2, b2, alpha=alpha,
                  name="finish_sample")

    head_shape_p = (depth, b_p, t_p, N_HEADS, D_V)
    head_shape_s = (depth, b_s, t_s, N_HEADS, D_V)
    return (y_p.reshape(b_p, t_p, d_model), y_s.reshape(b_s, t_s, d_model),
            k_p.reshape(head_shape_p), v_p.reshape(head_shape_p),
            k_s.reshape(head_shape_s), v_s.reshape(head_shape_s),
            gn_s.reshape(depth, b_s, t_s, N_GROUPS, GROUP_DIM))
```

```python
import functools
import math

import numpy as np
import jax
import jax.numpy as jnp
from jax import lax
from jax.experimental import pallas as pl
from jax.experimental.pallas import tpu as pltpu

F32 = jnp.float32
BF16 = jnp.bfloat16

CHUNK = 64
N_HEADS = 4
D_QK = 64
D_V = 2 * D_QK
D_ATT = N_HEADS * D_V
N_GROUPS = 4
GROUP_DIM = 128
D_GMLP = N_GROUPS * GROUP_DIM
GMLP_CHUNK = 128
NUM_BUCKETS = 32
MAX_DISTANCE = 1024
LN_EPS = 1e-5
NEG_INF = -1e30
QK_SCALE = D_QK ** -0.5
LOG2E = math.log2(math.e)
ONES_ROWS = 16
VT_ROWS = D_V + ONES_ROWS

ROW_TILE = 1024
FINISH_ROW_TILE = 512
FINISH_PART_QUARTERS = (2,)
ATT_TILE = 256
KEY_TILES_PER_TRIP = 2
CACHE_TILE = 2048
FFN_CHUNK = 1024
VMEM_LIMIT = 56 * 1024 * 1024
FUSED_VMEM_LIMIT = 60 * 1024 * 1024

_NT = (((1,), (1,)), ((), ()))


def _lambda_init(layer):
    return 0.8 - 0.6 * math.exp(-0.3 * layer)


def _bucket_of_distance(max_n):
    half = NUM_BUCKETS // 2
    max_exact = half // 2
    n = np.arange(max_n + 1)

    def table(dtype):
        nf = np.maximum(n, 1).astype(dtype)
        scaled = np.log(nf / dtype(max_exact)) / dtype(math.log(MAX_DISTANCE / max_exact)) * dtype(half - max_exact)
        large = np.minimum(max_exact + scaled.astype(np.int64), half - 1)
        return np.where(n < max_exact, n, large)

    t64 = table(np.float64)
    assert np.array_equal(t64, table(np.float32)), "bucket boundaries are precision sensitive"
    return t64


def _rel_bucket_static(rel, bucket_table):
    return (NUM_BUCKETS // 2 if rel > 0 else 0) + int(bucket_table[abs(rel)])


def _bias_tile(tab_ref, head, d0, nrows, ncols, bucket_table, *, keys_on_rows=False):
    row = lax.broadcasted_iota(jnp.int32, (nrows, ncols), 0)
    col = lax.broadcasted_iota(jnp.int32, (nrows, ncols), 1)
    if keys_on_rows:
        d = (row - col) + d0
        dmin, dmax = d0 - (ncols - 1), d0 + nrows - 1
    else:
        d = (col - row) + d0
        dmin, dmax = d0 - (nrows - 1), d0 + ncols - 1
    prev = _rel_bucket_static(dmin, bucket_table)
    val = jnp.full((nrows, ncols), tab_ref[prev, head] * LOG2E, F32)
    for dd in range(dmin + 1, dmax + 1):
        b = _rel_bucket_static(dd, bucket_table)
        if b != prev:
            val = jnp.where(d >= dd, tab_ref[b, head] * LOG2E, val)
            prev = b
    return val


def _bias_kernel(tab_ref, bp_ref, bs_ref, bn_ref, *, n_prompt, t, n_sample, tc, past, ts, bucket_table):
    key = lax.broadcasted_iota(jnp.int32, (t, t), 0)
    qry = lax.broadcasted_iota(jnp.int32, (t, t), 1)
    visible = (key // CHUNK) <= (qry // CHUNK)
    n_cache_tiles = past // tc
    for h in range(N_HEADS):
        for dl in range(n_prompt):
            tile = _bias_tile(tab_ref, h, -dl * t, t, t, bucket_table, keys_on_rows=True)
            if dl == 0:
                tile = jnp.where(visible, tile, NEG_INF)
            bp_ref[h, dl] = tile
        for i in range(n_sample):
            j = n_cache_tiles - n_sample + i
            bs_ref[h, i] = _bias_tile(tab_ref, h, j * tc - past, ts, tc, bucket_table)
        bn_ref[h] = _bias_tile(tab_ref, h, 0, ts, ts, bucket_table)


def _far_distance(bucket_table):
    n = len(bucket_table) - 1
    while n > 0 and bucket_table[n - 1] == bucket_table[-1]:
        n -= 1
    return n


def _num_varying_tiles(tile, bucket_table):
    far = _far_distance(bucket_table)
    n = 1
    while (n - 1) * tile + 1 < far:
        n += 1
    return n


def _gelu_tanh(x):
    c = math.sqrt(2.0 / math.pi)
    return 0.5 * x * (1.0 + jnp.tanh(c * (x + 0.044715 * (x * x * x))))


def _layer_norm(x, g, b):
    mu = jnp.mean(x, axis=-1, keepdims=True)
    xc = x - mu
    var = jnp.mean(xc * xc, axis=-1, keepdims=True)
    return xc * lax.rsqrt(var + LN_EPS) * g + b


def _store_heads(ref, x, n):
    rows = x.shape[0]
    for h in range(n):
        ref[pl.ds(h, rows, stride=n), :] = x[:, h * 128:(h + 1) * 128]


def _proj_kernel(x_ref, w_ref, lng_ref, lnb_ref, ws_ref, bsp_ref, *out_refs, chunk, att_tile, sample):
    if sample:
        q_ref, kf_ref, vf_ref, kb_ref, vb_ref, gm_ref, gn_ref = out_refs
    else:
        q_ref, kf_ref, vf_ref, kb_ref, vt_ref, gm_ref = out_refs
    tm = x_ref.shape[0]
    xb = x_ref[...].astype(BF16)

    def seg(lo, hi):
        return jnp.dot(xb, w_ref[:, lo:hi], preferred_element_type=F32)

    g_lin = seg(3 * D_ATT + D_GMLP, 3 * D_ATT + 2 * D_GMLP)
    u_lin = seg(3 * D_ATT, 3 * D_ATT + D_GMLP)

    lng = lng_ref[...]
    lnb = lnb_ref[...]
    g = _gelu_tanh(g_lin)
    gn = jnp.concatenate(
        [_layer_norm(g[:, i * GROUP_DIM:(i + 1) * GROUP_DIM],
                     lng[:, i * GROUP_DIM:(i + 1) * GROUP_DIM],
                     lnb[:, i * GROUP_DIM:(i + 1) * GROUP_DIM]) for i in range(N_GROUPS)], axis=1)
    if sample:
        _store_heads(gn_ref, gn, N_GROUPS)
    gnb = gn.astype(BF16)

    v = seg(2 * D_ATT, 3 * D_ATT)
    _store_heads(vf_ref, v, N_HEADS)
    if sample:
        vb_ref[...] = v.astype(BF16)
    else:
        ones = jnp.ones((ONES_ROWS, att_tile), F32)
        for j in range(tm // att_tile):
            vt = v[j * att_tile:(j + 1) * att_tile, :].T
            vt_ref[j] = jnp.concatenate(
                [part for h in range(N_HEADS) for part in (vt[h * D_V:(h + 1) * D_V], ones)],
                axis=0).astype(BF16)

    u = _gelu_tanh(u_lin)

    k = seg(D_ATT, 2 * D_ATT)
    _store_heads(kf_ref, k, N_HEADS)
    kb_ref[...] = k.astype(BF16)
    q_ref[...] = (seg(0, D_ATT) * (QK_SCALE * LOG2E)).astype(BF16)

    ri = lax.broadcasted_iota(jnp.int32, (chunk, chunk), 0)
    ci = lax.broadcasted_iota(jnp.int32, (chunk, chunk), 1)
    keep = (ri // CHUNK) >= (ci // CHUNK)
    ws = [jnp.where(keep, ws_ref[i, :chunk, :chunk], 0.0).astype(BF16) for i in range(N_GROUPS)]
    bsp = bsp_ref[...]
    for c in range(tm // chunk):
        r0 = c * chunk
        sp = jnp.concatenate(
            [jnp.dot(ws[i], gnb[r0:r0 + chunk, i * GROUP_DIM:(i + 1) * GROUP_DIM],
                     preferred_element_type=F32) for i in range(N_GROUPS)], axis=1)
        gm_ref[r0:r0 + chunk, :] = (u[r0:r0 + chunk, :] * (sp + bsp)).astype(BF16)


def _diff_lambda(lamv_ref, lam_init):
    lv = lamv_ref[...]
    s1 = jnp.sum(lv[0:1, :] * lv[1:2, :], axis=-1, keepdims=True)
    s2 = jnp.sum(lv[2:3, :] * lv[3:4, :], axis=-1, keepdims=True)
    return jnp.exp(s1) - jnp.exp(s2) + lam_init


def _stack_halves(qh):
    lane = lax.broadcasted_iota(jnp.int32, qh.shape, 1)
    qf = qh.astype(F32)
    first = jnp.where(lane < D_QK, qf, 0.0)
    second = jnp.where(lane >= D_QK, qf, 0.0)
    return jnp.concatenate([first, second], axis=0).astype(BF16)


def _attn_prompt_kernel(lamv_ref, gaint_ref, q_ref, k_ref, vt_ref, bias_ref, o_ref,
                        qz_ref, acc_ref, m_even, m_odd, s_even, s_odd, *, t, n_bias, lam_init):
    qi = pl.program_id(0)
    bb = pl.program_id(1)
    nb = pl.num_programs(1) - 1
    n_tiles = qi + 1
    has1 = bb < nb
    has2 = bb >= 1

    def pass1_head(s_ref, m_ref, h, kj, width):
        off = pl.multiple_of(kj * t, t)
        s = lax.dot_general(k_ref[pl.ds(off, width * t), h * D_V:(h + 1) * D_V], qz_ref[h], _NT,
                            preferred_element_type=F32)
        tiles = [bias_ref[h, jnp.minimum(qi - (kj + i), n_bias - 1)] for i in range(width)]
        s = s + jnp.concatenate([jnp.concatenate([b, b], axis=1) for b in tiles], axis=0)
        s_ref[h, pl.ds(off, width * t), :] = s
        m_ref[h] = jnp.maximum(m_ref[h], jnp.max(s, axis=0, keepdims=True))

    def pass2_head(s_ref, m_ref, h, kj, width):
        off = pl.multiple_of(kj * t, t)
        p = jnp.exp2(s_ref[h, pl.ds(off, width * t), :] - m_ref[h])
        vt = jnp.concatenate([vt_ref[kj + i, h * VT_ROWS:(h + 1) * VT_ROWS, :] for i in range(width)],
                             axis=1)
        acc_ref[h] = acc_ref[h] + jnp.dot(vt, p.astype(BF16), preferred_element_type=F32)

    def over_key_tiles(per_head_steps):
        def tiles(kj, width):
            for h in range(N_HEADS):
                for step in per_head_steps:
                    step(h, kj, width)

        widest = KEY_TILES_PER_TRIP

        def trip(i, carry):
            tiles(widest * i, widest)
            return carry

        lax.fori_loop(0, n_tiles // widest, trip, 0)
        done = (n_tiles // widest) * widest
        width = widest // 2
        while width >= 1:
            @pl.when((n_tiles & width) != 0)
            def _(width=width, done=done):
                tiles(done, width)

            done = done + (n_tiles & width)
            width //= 2

    def step(s_new, m_new, s_old, m_old):
        p1 = functools.partial(pass1_head, s_new, m_new)
        p2 = functools.partial(pass2_head, s_old, m_old)

        @pl.when(has1)
        def _():
            for h in range(N_HEADS):
                qz_ref[h] = _stack_halves(q_ref[:, h * D_V:(h + 1) * D_V])
            m_new[...] = jnp.full(m_new.shape, -jnp.inf, F32)

        @pl.when(jnp.logical_and(has1, has2))
        def _():
            over_key_tiles((p1, p2))

        @pl.when(jnp.logical_not(has2))
        def _():
            over_key_tiles((p1,))

        @pl.when(jnp.logical_not(has1))
        def _():
            over_key_tiles((p2,))

    @pl.when(has2)
    def _():
        acc_ref[...] = jnp.zeros(acc_ref.shape, F32)

    @pl.when(jnp.logical_not(has2))
    def _():
        o_ref[...] = jnp.zeros(o_ref.shape, o_ref.dtype)

    @pl.when(bb % 2 == 0)
    def _():
        step(s_even, m_even, s_odd, m_odd)

    @pl.when(bb % 2 == 1)
    def _():
        step(s_odd, m_odd, s_even, m_even)

    @pl.when(has2)
    def _():
        lam = _diff_lambda(lamv_ref, lam_init)
        for h in range(N_HEADS):
            acc = acc_ref[h]
            o = acc[:D_V] / acc[D_V:D_V + 1]
            od = o[:, :t] - lam * o[:, t:]
            ms = jnp.mean(od * od, axis=0, keepdims=True)
            y = od * lax.rsqrt(ms + LN_EPS) * gaint_ref[:, h:h + 1] * (1.0 - lam_init)
            o_ref[:, h * D_V:(h + 1) * D_V] = y.T.astype(BF16)


def _sample_attention_chunk(chunk, n_chunks, lamv_ref, gain_ref, q_ref, kn_ref, vn_ref, kc_ref, vc_ref,
                            bias_ref, biasn_ref, o_ref, qz_ref, m_ref, acc_ref, s_ref, *, n_varying, lam_init):
    ts = q_ref.shape[0]
    tc = kc_ref.shape[0] // N_HEADS

    def raise_max(h, s):
        m_old = m_ref[h]
        m_new = jnp.maximum(m_old, jnp.max(s, axis=-1, keepdims=True))
        m_ref[h] = m_new
        acc_ref[h] = acc_ref[h] * jnp.exp2(m_old - m_new)
        return m_new

    def weighted_values(p, v):
        v1 = jnp.concatenate([v, jnp.ones(v.shape, BF16)], axis=1)
        return jnp.dot(p.astype(BF16), v1, preferred_element_type=F32)

    def scores():
        @pl.when(chunk == 0)
        def _():
            for h in range(N_HEADS):
                qz_ref[h] = _stack_halves(q_ref[:, h * D_V:(h + 1) * D_V])
            m_ref[...] = jnp.full(m_ref.shape, -jnp.inf, F32)
            acc_ref[...] = jnp.zeros(acc_ref.shape, F32)

        bi = jnp.maximum(chunk - (n_chunks - n_varying - 1), 0)
        for h in range(N_HEADS):
            kh = kc_ref[pl.ds(h, tc, stride=N_HEADS), :].astype(BF16)
            s = lax.dot_general(qz_ref[h], kh, _NT, preferred_element_type=F32)
            bias = bias_ref[h, bi]
            s = s + jnp.concatenate([bias, bias], axis=0)
            s_ref[h] = s
            raise_max(h, s)

    def values(h):
        p = jnp.exp2(s_ref[h] - m_ref[h])
        vh = vc_ref[pl.ds(h, tc, stride=N_HEADS), :].astype(BF16)
        acc_ref[h] = acc_ref[h] + weighted_values(p, vh)

    def finish():
        @pl.when(chunk == n_chunks - 1)
        def _():
            lam = _diff_lambda(lamv_ref, lam_init)
            for h in range(N_HEADS):
                hs = slice(h * D_V, (h + 1) * D_V)
                s = lax.dot_general(qz_ref[h], kn_ref[:, hs], _NT, preferred_element_type=F32)
                bias = biasn_ref[h]
                s = s + jnp.concatenate([bias, bias], axis=0)
                m_new = raise_max(h, s)
                acc = acc_ref[h] + weighted_values(jnp.exp2(s - m_new), vn_ref[:, hs])
                o = acc[:, :D_V] / acc[:, D_V:D_V + 1]
                od = o[:ts] - lam * o[ts:]
                ms = jnp.mean(od * od, axis=-1, keepdims=True)
                y = od * lax.rsqrt(ms + LN_EPS) * gain_ref[h:h + 1, :] * (1.0 - lam_init)
                o_ref[:, hs] = y.astype(BF16)

    return scores, [functools.partial(values, h) for h in range(N_HEADS)], finish


_N_SAMPLE_INPUTS = 9


def _ffn_up(x1b, wu_ref, c):
    hc = jnp.dot(x1b, wu_ref[:, c * FFN_CHUNK:(c + 1) * FFN_CHUNK], preferred_element_type=F32)
    return jnp.square(jnp.maximum(hc, 0.0)).astype(BF16)


def _ffn_down(hc, wd_ref, c):
    return jnp.dot(hc, wd_ref[c * FFN_CHUNK:(c + 1) * FFN_CHUNK, :], preferred_element_type=F32)


def _finish_kernel(x_ref, a_ref, gm_ref, wo_ref, g1_ref, b1_ref, wu_ref, wd_ref, g2_ref, b2_ref, *rest,
                   alpha, sample_cfg):
    sample_values, sample_finish = [], None
    if sample_cfg is None:
        (y_ref,) = rest
    else:
        y_ref = rest[_N_SAMPLE_INPUTS]
        per_batch = sample_cfg["chunks_per_batch"]
        sample_scores, sample_values, sample_finish = _sample_attention_chunk(
            pl.program_id(0) % per_batch, per_batch, *rest[:_N_SAMPLE_INPUTS], *rest[_N_SAMPLE_INPUTS + 1:],
            n_varying=sample_cfg["n_varying"], lam_init=sample_cfg["lam_init"])
        sample_scores()

    tm = x_ref.shape[0]
    bounds = [0] + [tm * f // 4 for f in FINISH_PART_QUARTERS] + [tm]
    parts = [slice(lo, hi) for lo, hi in zip(bounds[:-1], bounds[1:])]

    def out_proj_ln(rs):
        mixed = jnp.concatenate([a_ref[rs, :], gm_ref[rs, :]], axis=1)
        r = jnp.dot(mixed, wo_ref[...], preferred_element_type=F32)
        return _layer_norm(alpha * x_ref[rs, :] + r, g1_ref[...], b1_ref[...])

    def ffn(x1, between=()):
        x1b = x1.astype(BF16)
        f = alpha * x1
        for c in range(wu_ref.shape[1] // FFN_CHUNK):
            f = f + _ffn_down(_ffn_up(x1b, wu_ref, c), wd_ref, c)
            if c < len(between):
                between[c]()
        return f

    x1 = out_proj_ln(parts[0])
    for j, rs in enumerate(parts):
        x1_next = out_proj_ln(parts[j + 1]) if j + 1 < len(parts) else None
        y_ref[rs, :] = _layer_norm(ffn(x1, sample_values if j == 0 else ()), g2_ref[...], b2_ref[...])
        x1 = x1_next
    if sample_finish is not None:
        sample_finish()


def _const_spec(shape):
    nd = len(shape)
    return pl.BlockSpec(shape, lambda *_: (0,) * nd, pipeline_mode=pl.Buffered(1))


def _params(*sem):
    return pltpu.CompilerParams(dimension_semantics=sem, vmem_limit_bytes=VMEM_LIMIT)


def _project(x2d, w_in, lng, lnb, w_spatial, bsp, *, chunk, att_tile, sample):
    n, d = x2d.shape
    tm = min(ROW_TILE, n)
    row = lambda width: pl.BlockSpec((tm, width), lambda i: (i, 0))
    per_head = pl.BlockSpec((tm * N_HEADS, D_V), lambda i: (i, 0))
    out_shape = [jax.ShapeDtypeStruct((n, D_ATT), BF16), jax.ShapeDtypeStruct((n * N_HEADS, D_V), F32),
                 jax.ShapeDtypeStruct((n * N_HEADS, D_V), F32), jax.ShapeDtypeStruct((n, D_ATT), BF16)]
    out_specs = [row(D_ATT), per_head, per_head, row(D_ATT)]
    if sample:
        out_shape += [jax.ShapeDtypeStruct((n, D_ATT), BF16), jax.ShapeDtypeStruct((n, D_GMLP), BF16),
                      jax.ShapeDtypeStruct((n * N_GROUPS, GROUP_DIM), F32)]
        out_specs += [row(D_ATT), row(D_GMLP), pl.BlockSpec((tm * N_GROUPS, GROUP_DIM), lambda i: (i, 0))]
    else:
        out_shape += [jax.ShapeDtypeStruct((n // att_tile, N_HEADS * VT_ROWS, att_tile), BF16),
                      jax.ShapeDtypeStruct((n, D_GMLP), BF16)]
        out_specs += [pl.BlockSpec((tm // att_tile, N_HEADS * VT_ROWS, att_tile), lambda i: (i, 0, 0)), row(D_GMLP)]
    return pl.pallas_call(
        functools.partial(_proj_kernel, chunk=chunk, att_tile=att_tile, sample=sample),
        grid=(n // tm,),
        in_specs=[row(d), _const_spec(w_in.shape), _const_spec(lng.shape), _const_spec(lnb.shape),
                  _const_spec(w_spatial.shape), _const_spec(bsp.shape)],
        out_specs=out_specs,
        out_shape=out_shape,
        compiler_params=_params("parallel"),
        name="proj_sample" if sample else "proj_prompt",
    )(x2d, w_in, lng, lnb, w_spatial, bsp)


def _finish(x2d, attn, gm, w_out, g1, b1, w_up, w_down, g2, b2, *, alpha, name, sample=None):
    n, d = x2d.shape
    tm = min(FINISH_ROW_TILE, n)
    nt = n // tm
    row = lambda width: pl.BlockSpec((tm, width), lambda i: (i, 0))
    in_specs = [row(d), row(D_ATT), row(D_GMLP), _const_spec(w_out.shape), _const_spec(g1.shape),
                _const_spec(b1.shape), _const_spec(w_up.shape), _const_spec(w_down.shape),
                _const_spec(g2.shape), _const_spec(b2.shape)]
    operands = [x2d, attn, gm, w_out, g1, b1, w_up, w_down, g2, b2]
    out_specs = [row(d)]
    out_shape = [jax.ShapeDtypeStruct((n, d), F32)]
    scratch = []
    sample_cfg = None
    if sample is not None:
        ck, cv = sample["cache_k"], sample["cache_v"]
        b_s, tc = ck.shape[0], sample["cache_tile"]
        per_batch = ck.shape[1] // (tc * N_HEADS)
        assert nt == b_s * per_batch, "one cache chunk per grid step"
        t_s = sample["q"].shape[0] // b_s
        new_spec = pl.BlockSpec((t_s, D_ATT), lambda i: (i // per_batch, 0))
        cache_spec = pl.BlockSpec((None, tc * N_HEADS, D_V), lambda i: (i // per_batch, i % per_batch, 0))
        small = [sample["lamv"], sample["gain"]]
        tiles = [sample["bias_cache"], sample["bias_new"]]
        in_specs += ([_const_spec(a.shape) for a in small] + [new_spec] * 3 + [cache_spec] * 2
                     + [_const_spec(a.shape) for a in tiles])
        operands += small + [sample["q"], sample["k_new"], sample["v_new"], ck, cv] + tiles
        out_specs.append(new_spec)
        out_shape.append(jax.ShapeDtypeStruct((b_s * t_s, D_ATT), BF16))
        scratch = [pltpu.VMEM((N_HEADS, 2 * t_s, D_V), BF16), pltpu.VMEM((N_HEADS, 2 * t_s, 1), F32),
                   pltpu.VMEM((N_HEADS, 2 * t_s, 2 * D_V), F32), pltpu.VMEM((N_HEADS, 2 * t_s, tc), F32)]
        sample_cfg = dict(chunks_per_batch=per_batch, n_varying=sample["n_varying"],
                          lam_init=sample["lam_init"])
    outs = pl.pallas_call(
        functools.partial(_finish_kernel, alpha=alpha, sample_cfg=sample_cfg),
        grid=(nt,),
        in_specs=in_specs,
        out_specs=out_specs,
        out_shape=out_shape,
        scratch_shapes=scratch,
        compiler_params=pltpu.CompilerParams(
            dimension_semantics=("arbitrary",),
            vmem_limit_bytes=FUSED_VMEM_LIMIT if sample is not None else VMEM_LIMIT),
        name=name,
    )(*operands)
    return outs if sample is not None else outs[0]


def kernel(x_prompt, x_sample, cache_k, cache_v, rel_bias_table, w_in, lambda_q1, lambda_k1, lambda_q2,
           lambda_k2, subln_gain, gmlp_ln_gain, gmlp_ln_bias, w_spatial, b_spatial, w_out, ln1_gain,
           ln1_bias, w_ffn_up, w_ffn_down, ln2_gain, ln2_bias):
    depth = w_in.shape[0]
    assert depth == 1, "single-layer step"
    b_p, t_p, d_model = x_prompt.shape
    b_s, t_s, _ = x_sample.shape
    past = cache_k.shape[2]
    alpha = (2.0 * depth) ** 0.25
    lam_init = _lambda_init(0)
    t = ATT_TILE
    tc = CACHE_TILE
    assert t_p % t == 0 and past % tc == 0 and t % CHUNK == 0 and t_s == CHUNK and past % GMLP_CHUNK == 0
    assert ROW_TILE % t == 0 and (b_p * t_p) % ROW_TILE == 0 and (b_s * t_s) % ROW_TILE == 0

    bucket_table = _bucket_of_distance(past + t_s)
    n_prompt_bias = _num_varying_tiles(t, bucket_table) + 1
    n_sample_var = _num_varying_tiles(tc, bucket_table) - 1
    n_sample_bias = n_sample_var + 1
    assert past // tc > n_sample_var

    bias_p, bias_s, bias_n = pl.pallas_call(
        functools.partial(_bias_kernel, n_prompt=n_prompt_bias, t=t, n_sample=n_sample_bias, tc=tc,
                          past=past, ts=t_s, bucket_table=bucket_table),
        in_specs=[pl.BlockSpec(memory_space=pltpu.SMEM)],
        out_shape=[jax.ShapeDtypeStruct((N_HEADS, n_prompt_bias, t, t), F32),
                   jax.ShapeDtypeStruct((N_HEADS, n_sample_bias, t_s, tc), F32),
                   jax.ShapeDtypeStruct((N_HEADS, t_s, t_s), F32)],
        compiler_params=pltpu.CompilerParams(vmem_limit_bytes=VMEM_LIMIT),
        name="rel_bias_tiles",
    )(rel_bias_table)

    w_in_b = w_in[0].astype(BF16)
    w_out_b = w_out[0].astype(BF16)
    w_up_b = w_ffn_up[0].astype(BF16)
    w_down_b = w_ffn_down[0].astype(BF16)
    lng = gmlp_ln_gain[0].reshape(1, D_GMLP)
    lnb = gmlp_ln_bias[0].reshape(1, D_GMLP)
    bsp = jnp.repeat(b_spatial[0].T, GROUP_DIM, axis=1)
    g1, b1 = ln1_gain[0].reshape(1, d_model), ln1_bias[0].reshape(1, d_model)
    g2, b2 = ln2_gain[0].reshape(1, d_model), ln2_bias[0].reshape(1, d_model)
    lamv = jnp.stack([lambda_q1[0], lambda_k1[0], lambda_q2[0], lambda_k2[0]], axis=0)
    gain = subln_gain[0]

    xp = x_prompt.reshape(b_p * t_p, d_model)
    q_p, k_p, v_p, kb_p, vt_p, gm_p = _project(xp, w_in_b, lng, lnb, w_spatial[0], bsp,
                                               chunk=GMLP_CHUNK, att_tile=t, sample=False)
    nq = t_p // t
    gain_t = gain.T
    attn_p = pl.pallas_call(
        functools.partial(_attn_prompt_kernel, t=t, n_bias=n_prompt_bias, lam_init=lam_init),
        grid=(nq, b_p + 1),
        in_specs=[_const_spec(lamv.shape), _const_spec(gain_t.shape),
                  pl.BlockSpec((t, D_ATT), lambda i, bb: (jnp.minimum(bb, b_p - 1) * nq + i, 0)),
                  pl.BlockSpec((t_p, D_ATT), lambda i, bb: (jnp.minimum(bb, b_p - 1), 0)),
                  pl.BlockSpec((nq, N_HEADS * VT_ROWS, t), lambda i, bb: (jnp.maximum(bb - 1, 0), 0, 0)),
                  _const_spec(bias_p.shape)],
        out_specs=pl.BlockSpec((t, D_ATT), lambda i, bb: (jnp.maximum(bb - 1, 0) * nq + i, 0)),
        out_shape=jax.ShapeDtypeStruct((b_p * t_p, D_ATT), BF16),
        scratch_shapes=[pltpu.VMEM((N_HEADS, 2 * t, D_V), BF16), pltpu.VMEM((N_HEADS, VT_ROWS, 2 * t), F32),
                        pltpu.VMEM((N_HEADS, 1, 2 * t), F32), pltpu.VMEM((N_HEADS, 1, 2 * t), F32),
                        pltpu.VMEM((N_HEADS, t_p, 2 * t), F32), pltpu.VMEM((N_HEADS, t_p, 2 * t), F32)],
        compiler_params=_params("arbitrary", "arbitrary"),
        name="attn_prompt",
    )(lamv, gain_t, q_p, kb_p, vt_p, bias_p)

    xs = x_sample.reshape(b_s * t_s, d_model)
    q_s, k_s, v_s, kb_s, vb_s, gm_s, gn_s = _project(xs, w_in_b, lng, lnb, w_spatial[0], bsp[:t_s],
                                                     chunk=t_s, att_tile=t, sample=True)
    sample = dict(lamv=lamv, gain=gain, q=q_s, k_new=kb_s, v_new=vb_s,
                  cache_k=cache_k.reshape(b_s, past * N_HEADS, D_V),
                  cache_v=cache_v.reshape(b_s, past * N_HEADS, D_V), cache_tile=tc,
                  bias_cache=bias_s, bias_new=bias_n, n_varying=n_sample_var, lam_init=lam_init)
    y_p, attn_s = _finish(xp, attn_p, gm_p, w_out_b, g1, b1, w_up_b, w_down_b, g2, b2, alpha=alpha,
                          name="finish_prompt_attn_sample", sample=sample)
    y_s = _finish(xs, attn_s, gm_s, w_out_b, g1, b1, w_up_b, w_down_b, g2, b2, alpha=alpha,
                  name="finish_sample")

    head_shape_p = (depth, b_p, t_p, N_HEADS, D_V)
    head_shape_s = (depth, b_s, t_s, N_HEADS, D_V)
    return (y_p.reshape(b_p, t_p, d_model), y_s.reshape(b_s, t_s, d_model),
            k_p.reshape(head_shape_p), v_p.reshape(head_shape_p),
            k_s.reshape(head_shape_s), v_s.reshape(head_shape_s),
            gn_s.reshape(depth, b_s, t_s, N_GROUPS, GROUP_DIM))
```

```python
import functools
import math

import numpy as np
import jax
import jax.numpy as jnp
from jax import lax
from jax.experimental import pallas as pl
from jax.experimental.pallas import tpu as pltpu

F32 = jnp.float32
BF16 = jnp.bfloat16

CHUNK = 64
N_HEADS = 4
D_QK = 64
D_V = 2 * D_QK
D_ATT = N_HEADS * D_V
N_GROUPS = 4
GROUP_DIM = 128
D_GMLP = N_GROUPS * GROUP_DIM
GMLP_CHUNK = 128
NUM_BUCKETS = 32
MAX_DISTANCE = 1024
LN_EPS = 1e-5
NEG_INF = -1e30
QK_SCALE = D_QK ** -0.5
LOG2E = math.log2(math.e)
ONES_ROWS = 16
VT_ROWS = D_V + ONES_ROWS

ROW_TILE = 1024
FINISH_ROW_TILE = 512
FINISH_PART_QUARTERS = (2,)
ATT_TILE = 256
KEY_TILES_PER_TRIP = 2
CACHE_TILE = 2048
FFN_CHUNK = 1024
VMEM_LIMIT = 56 * 1024 * 1024
FUSED_VMEM_LIMIT = 60 * 1024 * 1024

_NT = (((1,), (1,)), ((), ()))


def _lambda_init(layer):
    return 0.8 - 0.6 * math.exp(-0.3 * layer)


def _bucket_of_distance(max_n):
    half = NUM_BUCKETS // 2
    max_exact = half // 2
    n = np.arange(max_n + 1)

    def table(dtype):
        nf = np.maximum(n, 1).astype(dtype)
        scaled = np.log(nf / dtype(max_exact)) / dtype(math.log(MAX_DISTANCE / max_exact)) * dtype(half - max_exact)
        large = np.minimum(max_exact + scaled.astype(np.int64), half - 1)
        return np.where(n < max_exact, n, large)

    t64 = table(np.float64)
    assert np.array_equal(t64, table(np.float32)), "bucket boundaries are precision sensitive"
    return t64


def _rel_bucket_static(rel, bucket_table):
    return (NUM_BUCKETS // 2 if rel > 0 else 0) + int(bucket_table[abs(rel)])


def _bias_tile(tab_ref, head, d0, nrows, ncols, bucket_table, *, keys_on_rows=False):
    row = lax.broadcasted_iota(jnp.int32, (nrows, ncols), 0)
    col = lax.broadcasted_iota(jnp.int32, (nrows, ncols), 1)
    if keys_on_rows:
        d = (row - col) + d0
        dmin, dmax = d0 - (ncols - 1), d0 + nrows - 1
    else:
        d = (col - row) + d0
        dmin, dmax = d0 - (nrows - 1), d0 + ncols - 1
    prev = _rel_bucket_static(dmin, bucket_table)
    val = jnp.full((nrows, ncols), tab_ref[prev, head] * LOG2E, F32)
    for dd in range(dmin + 1, dmax + 1):
        b = _rel_bucket_static(dd, bucket_table)
        if b != prev:
            val = jnp.where(d >= dd, tab_ref[b, head] * LOG2E, val)
            prev = b
    return val


def _bias_kernel(tab_ref, bp_ref, bs_ref, bn_ref, *, n_prompt, t, n_sample, tc, past, ts, bucket_table):
    key = lax.broadcasted_iota(jnp.int32, (t, t), 0)
    qry = lax.broadcasted_iota(jnp.int32, (t, t), 1)
    visible = (key // CHUNK) <= (qry // CHUNK)
    n_cache_tiles = past // tc
    for h in range(N_HEADS):
        for dl in range(n_prompt):
            tile = _bias_tile(tab_ref, h, -dl * t, t, t, bucket_table, keys_on_rows=True)
            if dl == 0:
                tile = jnp.where(visible, tile, NEG_INF)
            bp_ref[h, dl] = tile
        for i in range(n_sample):
            j = n_cache_tiles - n_sample + i
            bs_ref[h, i] = _bias_tile(tab_ref, h, j * tc - past, ts, tc, bucket_table)
        bn_ref[h] = _bias_tile(tab_ref, h, 0, ts, ts, bucket_table)


def _far_distance(bucket_table):
    n = len(bucket_table) - 1
    while n > 0 and bucket_table[n - 1] == bucket_table[-1]:
        n -= 1
    return n


def _num_varying_tiles(tile, bucket_table):
    far = _far_distance(bucket_table)
    n = 1
    while (n - 1) * tile + 1 < far:
        n += 1
    return n


def _gelu_tanh(x):
    c = math.sqrt(2.0 / math.pi)
    return 0.5 * x * (1.0 + jnp.tanh(c * (x + 0.044715 * (x * x * x))))


def _layer_norm(x, g, b):
    mu = jnp.mean(x, axis=-1, keepdims=True)
    xc = x - mu
    var = jnp.mean(xc * xc, axis=-1, keepdims=True)
    return xc * lax.rsqrt(var + LN_EPS) * g + b


def _store_heads(ref, x, n):
    rows = x.shape[0]
    for h in range(n):
        ref[pl.ds(h, rows, stride=n), :] = x[:, h * 128:(h + 1) * 128]


def _proj_kernel(x_ref, w_ref, lng_ref, lnb_ref, ws_ref, bsp_ref, *out_refs, chunk, att_tile, sample):
    if sample:
        q_ref, kf_ref, vf_ref, kb_ref, vb_ref, gm_ref, gn_ref = out_refs
    else:
        q_ref, kf_ref, vf_ref, kb_ref, vt_ref, gm_ref = out_refs
    tm = x_ref.shape[0]
    xb = x_ref[...].astype(BF16)

    def seg(lo, hi):
        return jnp.dot(xb, w_ref[:, lo:hi], preferred_element_type=F32)

    g_lin = seg(3 * D_ATT + D_GMLP, 3 * D_ATT + 2 * D_GMLP)
    u_lin = seg(3 * D_ATT, 3 * D_ATT + D_GMLP)

    lng = lng_ref[...]
    lnb = lnb_ref[...]
    g = _gelu_tanh(g_lin)
    gn = jnp.concatenate(
        [_layer_norm(g[:, i * GROUP_DIM:(i + 1) * GROUP_DIM],
                     lng[:, i * GROUP_DIM:(i + 1) * GROUP_DIM],
                     lnb[:, i * GROUP_DIM:(i + 1) * GROUP_DIM]) for i in range(N_GROUPS)], axis=1)
    if sample:
        _store_heads(gn_ref, gn, N_GROUPS)
    gnb = gn.astype(BF16)

    v = seg(2 * D_ATT, 3 * D_ATT)
    _store_heads(vf_ref, v, N_HEADS)
    if sample:
        vb_ref[...] = v.astype(BF16)
    else:
        ones = jnp.ones((ONES_ROWS, att_tile), F32)
        for j in range(tm // att_tile):
            vt = v[j * att_tile:(j + 1) * att_tile, :].T
            vt_ref[j] = jnp.concatenate(
                [part for h in range(N_HEADS) for part in (vt[h * D_V:(h + 1) * D_V], ones)],
                axis=0).astype(BF16)

    u = _gelu_tanh(u_lin)

    k = seg(D_ATT, 2 * D_ATT)
    _store_heads(kf_ref, k, N_HEADS)
    kb_ref[...] = k.astype(BF16)
    q_ref[...] = (seg(0, D_ATT) * (QK_SCALE * LOG2E)).astype(BF16)

    ri = lax.broadcasted_iota(jnp.int32, (chunk, chunk), 0)
    ci = lax.broadcasted_iota(jnp.int32, (chunk, chunk), 1)
    keep = (ri // CHUNK) >= (ci // CHUNK)
    ws = [jnp.where(keep, ws_ref[i, :chunk, :chunk], 0.0).astype(BF16) for i in range(N_GROUPS)]
    bsp = bsp_ref[...]
    for c in range(tm // chunk):
        r0 = c * chunk
        sp = jnp.concatenate(
            [jnp.dot(ws[i], gnb[r0:r0 + chunk, i * GROUP_DIM:(i + 1) * GROUP_DIM],
                     preferred_element_type=F32) for i in range(N_GROUPS)], axis=1)
        gm_ref[r0:r0 + chunk, :] = (u[r0:r0 + chunk, :] * (sp + bsp)).astype(BF16)


def _diff_lambda(lamv_ref, lam_init):
    lv = lamv_ref[...]
    s1 = jnp.sum(lv[0:1, :] * lv[1:2, :], axis=-1, keepdims=True)
    s2 = jnp.sum(lv[2:3, :] * lv[3:4, :], axis=-1, keepdims=True)
    return jnp.exp(s1) - jnp.exp(s2) + lam_init


def _stack_halves(qh):
    lane = lax.broadcasted_iota(jnp.int32, qh.shape, 1)
    qf = qh.astype(F32)
    first = jnp.where(lane < D_QK, qf, 0.0)
    second = jnp.where(lane >= D_QK, qf, 0.0)
    return jnp.concatenate([first, second], axis=0).astype(BF16)


def _attn_prompt_kernel(lamv_ref, gaint_ref, q_ref, k_ref, vt_ref, bias_ref, o_ref,
                        qz_ref, acc_ref, m_even, m_odd, s_even, s_odd, *, t, n_bias, lam_init):
    qi = pl.program_id(0)
    bb = pl.program_id(1)
    nb = pl.num_programs(1) - 1
    n_tiles = qi + 1
    has1 = bb < nb
    has2 = bb >= 1

    def pass1_head(s_ref, m_ref, h, kj, width):
        off = pl.multiple_of(kj * t, t)
        s = lax.dot_general(k_ref[pl.ds(off, width * t), h * D_V:(h + 1) * D_V], qz_ref[h], _NT,
                            preferred_element_type=F32)
        tiles = [bias_ref[h, jnp.minimum(qi - (kj + i), n_bias - 1)] for i in range(width)]
        s = s + jnp.concatenate([jnp.concatenate([b, b], axis=1) for b in tiles], axis=0)
        s_ref[h, pl.ds(off, width * t), :] = s
        m_ref[h] = jnp.maximum(m_ref[h], jnp.max(s, axis=0, keepdims=True))

    def pass2_head(s_ref, m_ref, h, kj, width):
        off = pl.multiple_of(kj * t, t)
        p = jnp.exp2(s_ref[h, pl.ds(off, width * t), :] - m_ref[h])
        vt = jnp.concatenate([vt_ref[kj + i, h * VT_ROWS:(h + 1) * VT_ROWS, :] for i in range(width)],
                             axis=1)
        acc_ref[h] = acc_ref[h] + jnp.dot(vt, p.astype(BF16), preferred_element_type=F32)

    def over_key_tiles(per_head_steps):
        def tiles(kj, width):
            for h in range(N_HEADS):
                for step in per_head_steps:
                    step(h, kj, width)

        widest = KEY_TILES_PER_TRIP

        def trip(i, carry):
            tiles(widest * i, widest)
            return carry

        lax.fori_loop(0, n_tiles // widest, trip, 0)
        done = (n_tiles // widest) * widest
        width = widest // 2
        while width >= 1:
            @pl.when((n_tiles & width) != 0)
            def _(width=width, done=done):
                tiles(done, width)

            done = done + (n_tiles & width)
            width //= 2

    def step(s_new, m_new, s_old, m_old):
        p1 = functools.partial(pass1_head, s_new, m_new)
        p2 = functools.partial(pass2_head, s_old, m_old)

        @pl.when(has1)
        def _():
            for h in range(N_HEADS):
                qz_ref[h] = _stack_halves(q_ref[:, h * D_V:(h + 1) * D_V])
            m_new[...] = jnp.full(m_new.shape, -jnp.inf, F32)

        @pl.when(jnp.logical_and(has1, has2))
        def _():
            over_key_tiles((p1, p2))

        @pl.when(jnp.logical_not(has2))
        def _():
            over_key_tiles((p1,))

        @pl.when(jnp.logical_not(has1))
        def _():
            over_key_tiles((p2,))

    @pl.when(has2)
    def _():
        acc_ref[...] = jnp.zeros(acc_ref.shape, F32)

    @pl.when(jnp.logical_not(has2))
    def _():
        o_ref[...] = jnp.zeros(o_ref.shape, o_ref.dtype)

    @pl.when(bb % 2 == 0)
    def _():
        step(s_even, m_even, s_odd, m_odd)

    @pl.when(bb % 2 == 1)
    def _():
        step(s_odd, m_odd, s_even, m_even)

    @pl.when(has2)
    def _():
        lam = _diff_lambda(lamv_ref, lam_init)
        for h in range(N_HEADS):
            acc = acc_ref[h]
            o = acc[:D_V] / acc[D_V:D_V + 1]
            od = o[:, :t] - lam * o[:, t:]
            ms = jnp.mean(od * od, axis=0, keepdims=True)
            y = od * lax.rsqrt(ms + LN_EPS) * gaint_ref[:, h:h + 1] * (1.0 - lam_init)
            o_ref[:, h * D_V:(h + 1) * D_V] = y.T.astype(BF16)


def _sample_attention_chunk(chunk, n_chunks, lamv_ref, gain_ref, q_ref, kn_ref, vn_ref, kc_ref, vc_ref,
                            bias_ref, biasn_ref, o_ref, qz_ref, m_ref, acc_ref, s_ref, *, n_varying, lam_init):
    ts = q_ref.shape[0]
    tc = kc_ref.shape[0] // N_HEADS

    def raise_max(h, s):
        m_old = m_ref[h]
        m_new = jnp.maximum(m_old, jnp.max(s, axis=-1, keepdims=True))
        m_ref[h] = m_new
        acc_ref[h] = acc_ref[h] * jnp.exp2(m_old - m_new)
        return m_new

    def weighted_values(p, v):
        v1 = jnp.concatenate([v, jnp.ones(v.shape, BF16)], axis=1)
        return jnp.dot(p.astype(BF16), v1, preferred_element_type=F32)

    def scores():
        @pl.when(chunk == 0)
        def _():
            for h in range(N_HEADS):
                qz_ref[h] = _stack_halves(q_ref[:, h * D_V:(h + 1) * D_V])
            m_ref[...] = jnp.full(m_ref.shape, -jnp.inf, F32)
            acc_ref[...] = jnp.zeros(acc_ref.shape, F32)

        bi = jnp.maximum(chunk - (n_chunks - n_varying - 1), 0)
        for h in range(N_HEADS):
            kh = kc_ref[pl.ds(h, tc, stride=N_HEADS), :].astype(BF16)
            s = lax.dot_general(qz_ref[h], kh, _NT, preferred_element_type=F32)
            bias = bias_ref[h, bi]
            s = s + jnp.concatenate([bias, bias], axis=0)
            s_ref[h] = s
            raise_max(h, s)

    def values(h):
        p = jnp.exp2(s_ref[h] - m_ref[h])
        vh = vc_ref[pl.ds(h, tc, stride=N_HEADS), :].astype(BF16)
        acc_ref[h] = acc_ref[h] + weighted_values(p, vh)

    def finish():
        @pl.when(chunk == n_chunks - 1)
        def _():
            lam = _diff_lambda(lamv_ref, lam_init)
            heads = [slice(h * D_V, (h + 1) * D_V) for h in range(N_HEADS)]
            scores_new = []
            for h, hs in enumerate(heads):
                s = lax.dot_general(qz_ref[h], kn_ref[:, hs], _NT, preferred_element_type=F32)
                bias = biasn_ref[h]
                scores_new.append(s + jnp.concatenate([bias, bias], axis=0))
            accs = []
            for h, hs in enumerate(heads):
                m_new = raise_max(h, scores_new[h])
                accs.append(acc_ref[h] + weighted_values(jnp.exp2(scores_new[h] - m_new), vn_ref[:, hs]))
            for h, hs in enumerate(heads):
                acc = accs[h]
                o = acc[:, :D_V] / acc[:, D_V:D_V + 1]
                od = o[:ts] - lam * o[ts:]
                ms = jnp.mean(od * od, axis=-1, keepdims=True)
                y = od * lax.rsqrt(ms + LN_EPS) * gain_ref[h:h + 1, :] * (1.0 - lam_init)
                o_ref[:, hs] = y.astype(BF16)

    return scores, [functools.partial(values, h) for h in range(N_HEADS)], finish


_N_SAMPLE_INPUTS = 9


def _ffn_up(x1b, wu_ref, c):
    hc = jnp.dot(x1b, wu_ref[:, c * FFN_CHUNK:(c + 1) * FFN_CHUNK], preferred_element_type=F32)
    return jnp.square(jnp.maximum(hc, 0.0)).astype(BF16)


def _ffn_down(hc, wd_ref, c):
    return jnp.dot(hc, wd_ref[c * FFN_CHUNK:(c + 1) * FFN_CHUNK, :], preferred_element_type=F32)


def _finish_kernel(x_ref, a_ref, gm_ref, wo_ref, g1_ref, b1_ref, wu_ref, wd_ref, g2_ref, b2_ref, *rest,
                   alpha, sample_cfg):
    sample_values, sample_finish = [], None
    if sample_cfg is None:
        (y_ref,) = rest
    else:
        y_ref = rest[_N_SAMPLE_INPUTS]
        per_batch = sample_cfg["chunks_per_batch"]
        sample_scores, sample_values, sample_finish = _sample_attention_chunk(
            pl.program_id(0) % per_batch, per_batch, *rest[:_N_SAMPLE_INPUTS], *rest[_N_SAMPLE_INPUTS + 1:],
            n_varying=sample_cfg["n_varying"], lam_init=sample_cfg["lam_init"])
        sample_scores()

    tm = x_ref.shape[0]
    bounds = [0] + [tm * f // 4 for f in FINISH_PART_QUARTERS] + [tm]
    parts = [slice(lo, hi) for lo, hi in zip(bounds[:-1], bounds[1:])]

    def out_proj_ln(rs):
        mixed = jnp.concatenate([a_ref[rs, :], gm_ref[rs, :]], axis=1)
        r = jnp.dot(mixed, wo_ref[...], preferred_element_type=F32)
        return _layer_norm(alpha * x_ref[rs, :] + r, g1_ref[...], b1_ref[...])

    def ffn(x1, between=()):
        x1b = x1.astype(BF16)
        f = alpha * x1
        for c in range(wu_ref.shape[1] // FFN_CHUNK):
            f = f + _ffn_down(_ffn_up(x1b, wu_ref, c), wd_ref, c)
            if c < len(between):
                between[c]()
        return f

    x1 = out_proj_ln(parts[0])
    for j, rs in enumerate(parts):
        x1_next = out_proj_ln(parts[j + 1]) if j + 1 < len(parts) else None
        y_ref[rs, :] = _layer_norm(ffn(x1, sample_values if j == 0 else ()), g2_ref[...], b2_ref[...])
        x1 = x1_next
    if sample_finish is not None:
        sample_finish()


def _const_spec(shape):
    nd = len(shape)
    return pl.BlockSpec(shape, lambda *_: (0,) * nd, pipeline_mode=pl.Buffered(1))


def _params(*sem):
    return pltpu.CompilerParams(dimension_semantics=sem, vmem_limit_bytes=VMEM_LIMIT)


def _project(x2d, w_in, lng, lnb, w_spatial, bsp, *, chunk, att_tile, sample):
    n, d = x2d.shape
    tm = min(ROW_TILE, n)
    row = lambda width: pl.BlockSpec((tm, width), lambda i: (i, 0))
    per_head = pl.BlockSpec((tm * N_HEADS, D_V), lambda i: (i, 0))
    out_shape = [jax.ShapeDtypeStruct((n, D_ATT), BF16), jax.ShapeDtypeStruct((n * N_HEADS, D_V), F32),
                 jax.ShapeDtypeStruct((n * N_HEADS, D_V), F32), jax.ShapeDtypeStruct((n, D_ATT), BF16)]
    out_specs = [row(D_ATT), per_head, per_head, row(D_ATT)]
    if sample:
        out_shape += [jax.ShapeDtypeStruct((n, D_ATT), BF16), jax.ShapeDtypeStruct((n, D_GMLP), BF16),
                      jax.ShapeDtypeStruct((n * N_GROUPS, GROUP_DIM), F32)]
        out_specs += [row(D_ATT), row(D_GMLP), pl.BlockSpec((tm * N_GROUPS, GROUP_DIM), lambda i: (i, 0))]
    else:
        out_shape += [jax.ShapeDtypeStruct((n // att_tile, N_HEADS * VT_ROWS, att_tile), BF16),
                      jax.ShapeDtypeStruct((n, D_GMLP), BF16)]
        out_specs += [pl.BlockSpec((tm // att_tile, N_HEADS * VT_ROWS, att_tile), lambda i: (i, 0, 0)), row(D_GMLP)]
    return pl.pallas_call(
        functools.partial(_proj_kernel, chunk=chunk, att_tile=att_tile, sample=sample),
        grid=(n // tm,),
        in_specs=[row(d), _const_spec(w_in.shape), _const_spec(lng.shape), _const_spec(lnb.shape),
                  _const_spec(w_spatial.shape), _const_spec(bsp.shape)],
        out_specs=out_specs,
        out_shape=out_shape,
        compiler_params=_params("parallel"),
        name="proj_sample" if sample else "proj_prompt",
    )(x2d, w_in, lng, lnb, w_spatial, bsp)


def _finish(x2d, attn, gm, w_out, g1, b1, w_up, w_down, g2, b2, *, alpha, name, sample=None):
    n, d = x2d.shape
    tm = min(FINISH_ROW_TILE, n)
    nt = n // tm
    row = lambda width: pl.BlockSpec((tm, width), lambda i: (i, 0))
    in_specs = [row(d), row(D_ATT), row(D_GMLP), _const_spec(w_out.shape), _const_spec(g1.shape),
                _const_spec(b1.shape), _const_spec(w_up.shape), _const_spec(w_down.shape),
                _const_spec(g2.shape), _const_spec(b2.shape)]
    operands = [x2d, attn, gm, w_out, g1, b1, w_up, w_down, g2, b2]
    out_specs = [row(d)]
    out_shape = [jax.ShapeDtypeStruct((n, d), F32)]
    scratch = []
    sample_cfg = None
    if sample is not None:
        ck, cv = sample["cache_k"], sample["cache_v"]
        b_s, tc = ck.shape[0], sample["cache_tile"]
        per_batch = ck.shape[1] // (tc * N_HEADS)
        assert nt == b_s * per_batch, "one cache chunk per grid step"
        t_s = sample["q"].shape[0] // b_s
        new_spec = pl.BlockSpec((t_s, D_ATT), lambda i: (i // per_batch, 0))
        cache_spec = pl.BlockSpec((None, tc * N_HEADS, D_V), lambda i: (i // per_batch, i % per_batch, 0))
        small = [sample["lamv"], sample["gain"]]
        tiles = [sample["bias_cache"], sample["bias_new"]]
        in_specs += ([_const_spec(a.shape) for a in small] + [new_spec] * 3 + [cache_spec] * 2
                     + [_const_spec(a.shape) for a in tiles])
        operands += small + [sample["q"], sample["k_new"], sample["v_new"], ck, cv] + tiles
        out_specs.append(new_spec)
        out_shape.append(jax.ShapeDtypeStruct((b_s * t_s, D_ATT), BF16))
        scratch = [pltpu.VMEM((N_HEADS, 2 * t_s, D_V), BF16), pltpu.VMEM((N_HEADS, 2 * t_s, 1), F32),
                   pltpu.VMEM((N_HEADS, 2 * t_s, 2 * D_V), F32), pltpu.VMEM((N_HEADS, 2 * t_s, tc), F32)]
        sample_cfg = dict(chunks_per_batch=per_batch, n_varying=sample["n_varying"],
                          lam_init=sample["lam_init"])
    outs = pl.pallas_call(
        functools.partial(_finish_kernel, alpha=alpha, sample_cfg=sample_cfg),
        grid=(nt,),
        in_specs=in_specs,
        out_specs=out_specs,
        out_shape=out_shape,
        scratch_shapes=scratch,
        compiler_params=pltpu.CompilerParams(
            dimension_semantics=("arbitrary",),
            vmem_limit_bytes=FUSED_VMEM_LIMIT if sample is not None else VMEM_LIMIT),
        name=name,
    )(*operands)
    return outs if sample is not None else outs[0]


def kernel(x_prompt, x_sample, cache_k, cache_v, rel_bias_table, w_in, lambda_q1, lambda_k1, lambda_q2,
           lambda_k2, subln_gain, gmlp_ln_gain, gmlp_ln_bias, w_spatial, b_spatial, w_out, ln1_gain,
           ln1_bias, w_ffn_up, w_ffn_down, ln2_gain, ln2_bias):
    depth = w_in.shape[0]
    assert depth == 1, "single-layer step"
    b_p, t_p, d_model = x_prompt.shape
    b_s, t_s, _ = x_sample.shape
    past = cache_k.shape[2]
    alpha = (2.0 * depth) ** 0.25
    lam_init = _lambda_init(0)
    t = ATT_TILE
    tc = CACHE_TILE
    assert t_p % t == 0 and past % tc == 0 and t % CHUNK == 0 and t_s == CHUNK and past % GMLP_CHUNK == 0
    assert ROW_TILE % t == 0 and (b_p * t_p) % ROW_TILE == 0 and (b_s * t_s) % ROW_TILE == 0

    bucket_table = _bucket_of_distance(past + t_s)
    n_prompt_bias = _num_varying_tiles(t, bucket_table) + 1
    n_sample_var = _num_varying_tiles(tc, bucket_table) - 1
    n_sample_bias = n_sample_var + 1
    assert past // tc > n_sample_var

    bias_p, bias_s, bias_n = pl.pallas_call(
        functools.partial(_bias_kernel, n_prompt=n_prompt_bias, t=t, n_sample=n_sample_bias, tc=tc,
                          past=past, ts=t_s, bucket_table=bucket_table),
        in_specs=[pl.BlockSpec(memory_space=pltpu.SMEM)],
        out_shape=[jax.ShapeDtypeStruct((N_HEADS, n_prompt_bias, t, t), F32),
                   jax.ShapeDtypeStruct((N_HEADS, n_sample_bias, t_s, tc), F32),
                   jax.ShapeDtypeStruct((N_HEADS, t_s, t_s), F32)],
        compiler_params=pltpu.CompilerParams(vmem_limit_bytes=VMEM_LIMIT),
        name="rel_bias_tiles",
    )(rel_bias_table)

    w_in_b = w_in[0].astype(BF16)
    w_out_b = w_out[0].astype(BF16)
    w_up_b = w_ffn_up[0].astype(BF16)
    w_down_b = w_ffn_down[0].astype(BF16)
    lng = gmlp_ln_gain[0].reshape(1, D_GMLP)
    lnb = gmlp_ln_bias[0].reshape(1, D_GMLP)
    bsp = jnp.repeat(b_spatial[0].T, GROUP_DIM, axis=1)
    g1, b1 = ln1_gain[0].reshape(1, d_model), ln1_bias[0].reshape(1, d_model)
    g2, b2 = ln2_gain[0].reshape(1, d_model), ln2_bias[0].reshape(1, d_model)
    lamv = jnp.stack([lambda_q1[0], lambda_k1[0], lambda_q2[0], lambda_k2[0]], axis=0)
    gain = subln_gain[0]

    xp = x_prompt.reshape(b_p * t_p, d_model)
    q_p, k_p, v_p, kb_p, vt_p, gm_p = _project(xp, w_in_b, lng, lnb, w_spatial[0], bsp,
                                               chunk=GMLP_CHUNK, att_tile=t, sample=False)
    nq = t_p // t
    gain_t = gain.T
    attn_p = pl.pallas_call(
        functools.partial(_attn_prompt_kernel, t=t, n_bias=n_prompt_bias, lam_init=lam_init),
        grid=(nq, b_p + 1),
        in_specs=[_const_spec(lamv.shape), _const_spec(gain_t.shape),
                  pl.BlockSpec((t, D_ATT), lambda i, bb: (jnp.minimum(bb, b_p - 1) * nq + i, 0)),
                  pl.BlockSpec((t_p, D_ATT), lambda i, bb: (jnp.minimum(bb, b_p - 1), 0)),
                  pl.BlockSpec((nq, N_HEADS * VT_ROWS, t), lambda i, bb: (jnp.maximum(bb - 1, 0), 0, 0)),
                  _const_spec(bias_p.shape)],
        out_specs=pl.BlockSpec((t, D_ATT), lambda i, bb: (jnp.maximum(bb - 1, 0) * nq + i, 0)),
        out_shape=jax.ShapeDtypeStruct((b_p * t_p, D_ATT), BF16),
        scratch_shapes=[pltpu.VMEM((N_HEADS, 2 * t, D_V), BF16), pltpu.VMEM((N_HEADS, VT_ROWS, 2 * t), F32),
                        pltpu.VMEM((N_HEADS, 1, 2 * t), F32), pltpu.VMEM((N_HEADS, 1, 2 * t), F32),
                        pltpu.VMEM((N_HEADS, t_p, 2 * t), F32), pltpu.VMEM((N_HEADS, t_p, 2 * t), F32)],
        compiler_params=_params("arbitrary", "arbitrary"),
        name="attn_prompt",
    )(lamv, gain_t, q_p, kb_p, vt_p, bias_p)

    xs = x_sample.reshape(b_s * t_s, d_model)
    q_s, k_s, v_s, kb_s, vb_s, gm_s, gn_s = _project(xs, w_in_b, lng, lnb, w_spatial[0], bsp[:t_s],
                                                     chunk=t_s, att_tile=t, sample=True)
    sample = dict(lamv=lamv, gain=gain, q=q_s, k_new=kb_s, v_new=vb_s,
                  cache_k=cache_k.reshape(b_s, past * N_HEADS, D_V),
                  cache_v=cache_v.reshape(b_s, past * N_HEADS, D_V), cache_tile=tc,
                  bias_cache=bias_s, bias_new=bias_n, n_varying=n_sample_var, lam_init=lam_init)
    y_p, attn_s = _finish(xp, attn_p, gm_p, w_out_b, g1, b1, w_up_b, w_down_b, g2, b2, alpha=alpha,
                          name="finish_prompt_attn_sample", sample=sample)
    y_s = _finish(xs, attn_s, gm_s, w_out_b, g1, b1, w_up_b, w_down_b, g2, b2, alpha=alpha,
                  name="finish_sample")

    head_shape_p = (depth, b_p, t_p, N_HEADS, D_V)
    head_shape_s = (depth, b_s, t_s, N_HEADS, D_V)
    return (y_p.reshape(b_p, t_p, d_model), y_s.reshape(b_s, t_s, d_model),
            k_p.reshape(head_shape_p), v_p.reshape(head_shape_p),
            k_s.reshape(head_shape_s), v_s.reshape(head_shape_s),
            gn_s.reshape(depth, b_s, t_s, N_GROUPS, GROUP_DIM))
```

```python
import functools
import math

import numpy as np
import jax
import jax.numpy as jnp
from jax import lax
from jax.experimental import pallas as pl
from jax.experimental.pallas import tpu as pltpu

F32 = jnp.float32
BF16 = jnp.bfloat16

CHUNK = 64
N_HEADS = 4
D_QK = 64
D_V = 2 * D_QK
D_ATT = N_HEADS * D_V
N_GROUPS = 4
GROUP_DIM = 128
D_GMLP = N_GROUPS * GROUP_DIM
GMLP_CHUNK = 128
NUM_BUCKETS = 32
MAX_DISTANCE = 1024
LN_EPS = 1e-5
NEG_INF = -1e30
QK_SCALE = D_QK ** -0.5
LOG2E = math.log2(math.e)
ONES_ROWS = 16
VT_ROWS = D_V + ONES_ROWS

ROW_TILE = 1024
FINISH_ROW_TILE = 512
FINISH_PART_QUARTERS = (2,)
ATT_TILE = 256
KEY_TILES_PER_TRIP = 2
CACHE_TILE = 2048
FFN_CHUNK = 1024
VMEM_LIMIT = 56 * 1024 * 1024
FUSED_VMEM_LIMIT = 60 * 1024 * 1024

_NT = (((1,), (1,)), ((), ()))


def _lambda_init(layer):
    return 0.8 - 0.6 * math.exp(-0.3 * layer)


def _bucket_of_distance(max_n):
    half = NUM_BUCKETS // 2
    max_exact = half // 2
    n = np.arange(max_n + 1)

    def table(dtype):
        nf = np.maximum(n, 1).astype(dtype)
        scaled = np.log(nf / dtype(max_exact)) / dtype(math.log(MAX_DISTANCE / max_exact)) * dtype(half - max_exact)
        large = np.minimum(max_exact + scaled.astype(np.int64), half - 1)
        return np.where(n < max_exact, n, large)

    t64 = table(np.float64)
    assert np.array_equal(t64, table(np.float32)), "bucket boundaries are precision sensitive"
    return t64


def _rel_bucket_static(rel, bucket_table):
    return (NUM_BUCKETS // 2 if rel > 0 else 0) + int(bucket_table[abs(rel)])


def _bias_tile(tab_ref, head, d0, nrows, ncols, bucket_table, *, keys_on_rows=False):
    row = lax.broadcasted_iota(jnp.int32, (nrows, ncols), 0)
    col = lax.broadcasted_iota(jnp.int32, (nrows, ncols), 1)
    if keys_on_rows:
        d = (row - col) + d0
        dmin, dmax = d0 - (ncols - 1), d0 + nrows - 1
    else:
        d = (col - row) + d0
        dmin, dmax = d0 - (nrows - 1), d0 + ncols - 1
    prev = _rel_bucket_static(dmin, bucket_table)
    val = jnp.full((nrows, ncols), tab_ref[prev, head] * LOG2E, F32)
    for dd in range(dmin + 1, dmax + 1):
        b = _rel_bucket_static(dd, bucket_table)
        if b != prev:
            val = jnp.where(d >= dd, tab_ref[b, head] * LOG2E, val)
            prev = b
    return val


def _bias_kernel(tab_ref, bp_ref, bs_ref, bn_ref, *, n_prompt, t, n_sample, tc, past, ts, bucket_table):
    key = lax.broadcasted_iota(jnp.int32, (t, t), 0)
    qry = lax.broadcasted_iota(jnp.int32, (t, t), 1)
    visible = (key // CHUNK) <= (qry // CHUNK)
    n_cache_tiles = past // tc
    for h in range(N_HEADS):
        for dl in range(n_prompt):
            tile = _bias_tile(tab_ref, h, -dl * t, t, t, bucket_table, keys_on_rows=True)
            if dl == 0:
                tile = jnp.where(visible, tile, NEG_INF)
            bp_ref[h, dl] = tile
        for i in range(n_sample):
            j = n_cache_tiles - n_sample + i
            bs_ref[h, i] = _bias_tile(tab_ref, h, j * tc - past, ts, tc, bucket_table)
        bn_ref[h] = _bias_tile(tab_ref, h, 0, ts, ts, bucket_table)


def _far_distance(bucket_table):
    n = len(bucket_table) - 1
    while n > 0 and bucket_table[n - 1] == bucket_table[-1]:
        n -= 1
    return n


def _num_varying_tiles(tile, bucket_table):
    far = _far_distance(bucket_table)
    n = 1
    while (n - 1) * tile + 1 < far:
        n += 1
    return n


def _gelu_tanh(x):
    c = math.sqrt(2.0 / math.pi)
    return 0.5 * x * (1.0 + jnp.tanh(c * (x + 0.044715 * (x * x * x))))


def _layer_norm(x, g, b):
    mu = jnp.mean(x, axis=-1, keepdims=True)
    xc = x - mu
    var = jnp.mean(xc * xc, axis=-1, keepdims=True)
    return xc * lax.rsqrt(var + LN_EPS) * g + b


def _store_heads(ref, x, n):
    rows = x.shape[0]
    for h in range(n):
        ref[pl.ds(h, rows, stride=n), :] = x[:, h * 128:(h + 1) * 128]


def _proj_kernel(x_ref, w_ref, lng_ref, lnb_ref, ws_ref, bsp_ref, *out_refs, chunk, att_tile, sample):
    if sample:
        q_ref, kf_ref, vf_ref, kb_ref, vb_ref, gm_ref, gn_ref = out_refs
    else:
        q_ref, kf_ref, vf_ref, kb_ref, vt_ref, gm_ref = out_refs
    tm = x_ref.shape[0]
    xb = x_ref[...].astype(BF16)

    def seg(lo, hi):
        return jnp.dot(xb, w_ref[:, lo:hi], preferred_element_type=F32)

    g_lin = seg(3 * D_ATT + D_GMLP, 3 * D_ATT + 2 * D_GMLP)
    u_lin = seg(3 * D_ATT, 3 * D_ATT + D_GMLP)

    lng = lng_ref[...]
    lnb = lnb_ref[...]
    g = _gelu_tanh(g_lin)
    gn = jnp.concatenate(
        [_layer_norm(g[:, i * GROUP_DIM:(i + 1) * GROUP_DIM],
                     lng[:, i * GROUP_DIM:(i + 1) * GROUP_DIM],
                     lnb[:, i * GROUP_DIM:(i + 1) * GROUP_DIM]) for i in range(N_GROUPS)], axis=1)
    if sample:
        _store_heads(gn_ref, gn, N_GROUPS)
    gnb = gn.astype(BF16)

    v = seg(2 * D_ATT, 3 * D_ATT)
    _store_heads(vf_ref, v, N_HEADS)
    if sample:
        vb_ref[...] = v.astype(BF16)
    else:
        ones = jnp.ones((ONES_ROWS, att_tile), F32)
        for j in range(tm // att_tile):
            vt = v[j * att_tile:(j + 1) * att_tile, :].T
            vt_ref[j] = jnp.concatenate(
                [part for h in range(N_HEADS) for part in (vt[h * D_V:(h + 1) * D_V], ones)],
                axis=0).astype(BF16)

    u = _gelu_tanh(u_lin)

    k = seg(D_ATT, 2 * D_ATT)
    _store_heads(kf_ref, k, N_HEADS)
    kb_ref[...] = k.astype(BF16)
    q_ref[...] = (seg(0, D_ATT) * (QK_SCALE * LOG2E)).astype(BF16)

    ri = lax.broadcasted_iota(jnp.int32, (chunk, chunk), 0)
    ci = lax.broadcasted_iota(jnp.int32, (chunk, chunk), 1)
    keep = (ri // CHUNK) >= (ci // CHUNK)
    ws = [jnp.where(keep, ws_ref[i, :chunk, :chunk], 0.0).astype(BF16) for i in range(N_GROUPS)]
    bsp = bsp_ref[...]
    for c in range(tm // chunk):
        r0 = c * chunk
        sp = jnp.concatenate(
            [jnp.dot(ws[i], gnb[r0:r0 + chunk, i * GROUP_DIM:(i + 1) * GROUP_DIM],
                     preferred_element_type=F32) for i in range(N_GROUPS)], axis=1)
        gm_ref[r0:r0 + chunk, :] = (u[r0:r0 + chunk, :] * (sp + bsp)).astype(BF16)


def _diff_lambda(lamv_ref, lam_init):
    lv = lamv_ref[...]
    s1 = jnp.sum(lv[0:1, :] * lv[1:2, :], axis=-1, keepdims=True)
    s2 = jnp.sum(lv[2:3, :] * lv[3:4, :], axis=-1, keepdims=True)
    return jnp.exp(s1) - jnp.exp(s2) + lam_init


def _stack_halves(qh):
    lane = lax.broadcasted_iota(jnp.int32, qh.shape, 1)
    qf = qh.astype(F32)
    first = jnp.where(lane < D_QK, qf, 0.0)
    second = jnp.where(lane >= D_QK, qf, 0.0)
    return jnp.concatenate([first, second], axis=0).astype(BF16)


def _attn_prompt_kernel(lamv_ref, gaint_ref, q_ref, k_ref, vt_ref, bias_ref, o_ref,
                        qz_ref, acc_ref, m_even, m_odd, s_even, s_odd, *, t, n_bias, lam_init):
    qi = pl.program_id(0)
    bb = pl.program_id(1)
    nb = pl.num_programs(1) - 1
    n_tiles = qi + 1
    has1 = bb < nb
    has2 = bb >= 1

    def pass1_head(s_ref, m_ref, h, kj, width):
        off = pl.multiple_of(kj * t, t)
        s = lax.dot_general(k_ref[pl.ds(off, width * t), h * D_V:(h + 1) * D_V], qz_ref[h], _NT,
                            preferred_element_type=F32)
        tiles = [bias_ref[h, jnp.minimum(qi - (kj + i), n_bias - 1)] for i in range(width)]
        s = s + jnp.concatenate([jnp.concatenate([b, b], axis=1) for b in tiles], axis=0)
        s_ref[h, pl.ds(off, width * t), :] = s
        m_ref[h] = jnp.maximum(m_ref[h], jnp.max(s, axis=0, keepdims=True))

    def pass2_head(s_ref, m_ref, h, kj, width):
        off = pl.multiple_of(kj * t, t)
        p = jnp.exp2(s_ref[h, pl.ds(off, width * t), :] - m_ref[h])
        vt = jnp.concatenate([vt_ref[kj + i, h * VT_ROWS:(h + 1) * VT_ROWS, :] for i in range(width)],
                             axis=1)
        acc_ref[h] = acc_ref[h] + jnp.dot(vt, p.astype(BF16), preferred_element_type=F32)

    def over_key_tiles(per_head_steps):
        def tiles(kj, width):
            for h in range(N_HEADS):
                for step in per_head_steps:
                    step(h, kj, width)

        widest = KEY_TILES_PER_TRIP

        def trip(i, carry):
            tiles(widest * i, widest)
            return carry

        lax.fori_loop(0, n_tiles // widest, trip, 0)
        done = (n_tiles // widest) * widest
        width = widest // 2
        while width >= 1:
            @pl.when((n_tiles & width) != 0)
            def _(width=width, done=done):
                tiles(done, width)

            done = done + (n_tiles & width)
            width //= 2

    def step(s_new, m_new, s_old, m_old):
        p1 = functools.partial(pass1_head, s_new, m_new)
        p2 = functools.partial(pass2_head, s_old, m_old)

        @pl.when(has1)
        def _():
            for h in range(N_HEADS):
                qz_ref[h] = _stack_halves(q_ref[:, h * D_V:(h + 1) * D_V])
            m_new[...] = jnp.full(m_new.shape, -jnp.inf, F32)

        @pl.when(jnp.logical_and(has1, has2))
        def _():
            over_key_tiles((p1, p2))

        @pl.when(jnp.logical_not(has2))
        def _():
            over_key_tiles((p1,))

        @pl.when(jnp.logical_not(has1))
        def _():
            over_key_tiles((p2,))

    @pl.when(has2)
    def _():
        acc_ref[...] = jnp.zeros(acc_ref.shape, F32)

    @pl.when(jnp.logical_not(has2))
    def _():
        o_ref[...] = jnp.zeros(o_ref.shape, o_ref.dtype)

    @pl.when(bb % 2 == 0)
    def _():
        step(s_even, m_even, s_odd, m_odd)

    @pl.when(bb % 2 == 1)
    def _():
        step(s_odd, m_odd, s_even, m_even)

    @pl.when(has2)
    def _():
        lam = _diff_lambda(lamv_ref, lam_init)
        for h in range(N_HEADS):
            acc = acc_ref[h]
            o = acc[:D_V] / acc[D_V:D_V + 1]
            od = o[:, :t] - lam * o[:, t:]
            ms = jnp.mean(od * od, axis=0, keepdims=True)
            y = od * lax.rsqrt(ms + LN_EPS) * gaint_ref[:, h:h + 1] * (1.0 - lam_init)
            o_ref[:, h * D_V:(h + 1) * D_V] = y.T.astype(BF16)


def _sample_attention_chunk(chunk, n_chunks, lamv_ref, gain_ref, q_ref, kn_ref, vn_ref, kc_ref, vc_ref,
                            bias_ref, biasn_ref, o_ref, qz_ref, m_ref, acc_ref, s_ref, *, n_varying, lam_init):
    ts = q_ref.shape[0]
    tc = kc_ref.shape[0] // N_HEADS

    def raise_max(h, s):
        m_old = m_ref[h]
        m_new = jnp.maximum(m_old, jnp.max(s, axis=-1, keepdims=True))
        m_ref[h] = m_new
        acc_ref[h] = acc_ref[h] * jnp.exp2(m_old - m_new)
        return m_new

    def weighted_values(p, v):
        v1 = jnp.concatenate([v, jnp.ones(v.shape, BF16)], axis=1)
        return jnp.dot(p.astype(BF16), v1, preferred_element_type=F32)

    def begin():
        @pl.when(chunk == 0)
        def _():
            for h in range(N_HEADS):
                qz_ref[h] = _stack_halves(q_ref[:, h * D_V:(h + 1) * D_V])
            m_ref[...] = jnp.full(m_ref.shape, -jnp.inf, F32)
            acc_ref[...] = jnp.zeros(acc_ref.shape, F32)

    def scores(h):
        bi = jnp.maximum(chunk - (n_chunks - n_varying - 1), 0)
        kh = kc_ref[pl.ds(h, tc, stride=N_HEADS), :].astype(BF16)
        s = lax.dot_general(qz_ref[h], kh, _NT, preferred_element_type=F32)
        bias = bias_ref[h, bi]
        s = s + jnp.concatenate([bias, bias], axis=0)
        s_ref[h] = s
        raise_max(h, s)

    def values(h):
        p = jnp.exp2(s_ref[h] - m_ref[h])
        vh = vc_ref[pl.ds(h, tc, stride=N_HEADS), :].astype(BF16)
        acc_ref[h] = acc_ref[h] + weighted_values(p, vh)

    def finish():
        @pl.when(chunk == n_chunks - 1)
        def _():
            lam = _diff_lambda(lamv_ref, lam_init)
            heads = [slice(h * D_V, (h + 1) * D_V) for h in range(N_HEADS)]
            scores_new = []
            for h, hs in enumerate(heads):
                s = lax.dot_general(qz_ref[h], kn_ref[:, hs], _NT, preferred_element_type=F32)
                bias = biasn_ref[h]
                scores_new.append(s + jnp.concatenate([bias, bias], axis=0))
            accs = []
            for h, hs in enumerate(heads):
                m_new = raise_max(h, scores_new[h])
                accs.append(acc_ref[h] + weighted_values(jnp.exp2(scores_new[h] - m_new), vn_ref[:, hs]))
            for h, hs in enumerate(heads):
                acc = accs[h]
                o = acc[:, :D_V] / acc[:, D_V:D_V + 1]
                od = o[:ts] - lam * o[ts:]
                ms = jnp.mean(od * od, axis=-1, keepdims=True)
                y = od * lax.rsqrt(ms + LN_EPS) * gain_ref[h:h + 1, :] * (1.0 - lam_init)
                o_ref[:, hs] = y.astype(BF16)

    per_head = lambda fn: [functools.partial(fn, h) for h in range(N_HEADS)]
    return begin, per_head(scores), per_head(values), finish


_N_SAMPLE_INPUTS = 9


def _ffn_up(x1b, wu_ref, c):
    hc = jnp.dot(x1b, wu_ref[:, c * FFN_CHUNK:(c + 1) * FFN_CHUNK], preferred_element_type=F32)
    return jnp.square(jnp.maximum(hc, 0.0)).astype(BF16)


def _ffn_down(hc, wd_ref, c):
    return jnp.dot(hc, wd_ref[c * FFN_CHUNK:(c + 1) * FFN_CHUNK, :], preferred_element_type=F32)


def _finish_kernel(x_ref, a_ref, gm_ref, wo_ref, g1_ref, b1_ref, wu_ref, wd_ref, g2_ref, b2_ref, *rest,
                   alpha, sample_cfg):
    between = {}
    sample_finish = None
    if sample_cfg is None:
        (y_ref,) = rest
    else:
        y_ref = rest[_N_SAMPLE_INPUTS]
        per_batch = sample_cfg["chunks_per_batch"]
        sample_begin, between[0], between[1], sample_finish = _sample_attention_chunk(
            pl.program_id(0) % per_batch, per_batch, *rest[:_N_SAMPLE_INPUTS], *rest[_N_SAMPLE_INPUTS + 1:],
            n_varying=sample_cfg["n_varying"], lam_init=sample_cfg["lam_init"])
        sample_begin()

    tm = x_ref.shape[0]
    bounds = [0] + [tm * f // 4 for f in FINISH_PART_QUARTERS] + [tm]
    parts = [slice(lo, hi) for lo, hi in zip(bounds[:-1], bounds[1:])]

    def out_proj_ln(rs):
        mixed = jnp.concatenate([a_ref[rs, :], gm_ref[rs, :]], axis=1)
        r = jnp.dot(mixed, wo_ref[...], preferred_element_type=F32)
        return _layer_norm(alpha * x_ref[rs, :] + r, g1_ref[...], b1_ref[...])

    def ffn(x1, between=()):
        x1b = x1.astype(BF16)
        f = alpha * x1
        for c in range(wu_ref.shape[1] // FFN_CHUNK):
            f = f + _ffn_down(_ffn_up(x1b, wu_ref, c), wd_ref, c)
            if c < len(between):
                between[c]()
        return f

    x1 = out_proj_ln(parts[0])
    for j, rs in enumerate(parts):
        x1_next = out_proj_ln(parts[j + 1]) if j + 1 < len(parts) else None
        y_ref[rs, :] = _layer_norm(ffn(x1, between.get(j, ())), g2_ref[...], b2_ref[...])
        x1 = x1_next
    if sample_finish is not None:
        sample_finish()


def _const_spec(shape):
    nd = len(shape)
    return pl.BlockSpec(shape, lambda *_: (0,) * nd, pipeline_mode=pl.Buffered(1))


def _params(*sem):
    return pltpu.CompilerParams(dimension_semantics=sem, vmem_limit_bytes=VMEM_LIMIT)


def _project(x2d, w_in, lng, lnb, w_spatial, bsp, *, chunk, att_tile, sample):
    n, d = x2d.shape
    tm = min(ROW_TILE, n)
    row = lambda width: pl.BlockSpec((tm, width), lambda i: (i, 0))
    per_head = pl.BlockSpec((tm * N_HEADS, D_V), lambda i: (i, 0))
    out_shape = [jax.ShapeDtypeStruct((n, D_ATT), BF16), jax.ShapeDtypeStruct((n * N_HEADS, D_V), F32),
                 jax.ShapeDtypeStruct((n * N_HEADS, D_V), F32), jax.ShapeDtypeStruct((n, D_ATT), BF16)]
    out_specs = [row(D_ATT), per_head, per_head, row(D_ATT)]
    if sample:
        out_shape += [jax.ShapeDtypeStruct((n, D_ATT), BF16), jax.ShapeDtypeStruct((n, D_GMLP), BF16),
                      jax.ShapeDtypeStruct((n * N_GROUPS, GROUP_DIM), F32)]
        out_specs += [row(D_ATT), row(D_GMLP), pl.BlockSpec((tm * N_GROUPS, GROUP_DIM), lambda i: (i, 0))]
    else:
        out_shape += [jax.ShapeDtypeStruct((n // att_tile, N_HEADS * VT_ROWS, att_tile), BF16),
                      jax.ShapeDtypeStruct((n, D_GMLP), BF16)]
        out_specs += [pl.BlockSpec((tm // att_tile, N_HEADS * VT_ROWS, att_tile), lambda i: (i, 0, 0)), row(D_GMLP)]
    return pl.pallas_call(
        functools.partial(_proj_kernel, chunk=chunk, att_tile=att_tile, sample=sample),
        grid=(n // tm,),
        in_specs=[row(d), _const_spec(w_in.shape), _const_spec(lng.shape), _const_spec(lnb.shape),
                  _const_spec(w_spatial.shape), _const_spec(bsp.shape)],
        out_specs=out_specs,
        out_shape=out_shape,
        compiler_params=_params("parallel"),
        name="proj_sample" if sample else "proj_prompt",
    )(x2d, w_in, lng, lnb, w_spatial, bsp)


def _finish(x2d, attn, gm, w_out, g1, b1, w_up, w_down, g2, b2, *, alpha, name, sample=None):
    n, d = x2d.shape
    tm = min(FINISH_ROW_TILE, n)
    nt = n // tm
    row = lambda width: pl.BlockSpec((tm, width), lambda i: (i, 0))
    in_specs = [row(d), row(D_ATT), row(D_GMLP), _const_spec(w_out.shape), _const_spec(g1.shape),
                _const_spec(b1.shape), _const_spec(w_up.shape), _const_spec(w_down.shape),
                _const_spec(g2.shape), _const_spec(b2.shape)]
    operands = [x2d, attn, gm, w_out, g1, b1, w_up, w_down, g2, b2]
    out_specs = [row(d)]
    out_shape = [jax.ShapeDtypeStruct((n, d), F32)]
    scratch = []
    sample_cfg = None
    if sample is not None:
        ck, cv = sample["cache_k"], sample["cache_v"]
        b_s, tc = ck.shape[0], sample["cache_tile"]
        per_batch = ck.shape[1] // (tc * N_HEADS)
        assert nt == b_s * per_batch, "one cache chunk per grid step"
        t_s = sample["q"].shape[0] // b_s
        new_spec = pl.BlockSpec((t_s, D_ATT), lambda i: (i // per_batch, 0))
        cache_spec = pl.BlockSpec((None, tc * N_HEADS, D_V), lambda i: (i // per_batch, i % per_batch, 0))
        small = [sample["lamv"], sample["gain"]]
        tiles = [sample["bias_cache"], sample["bias_new"]]
        in_specs += ([_const_spec(a.shape) for a in small] + [new_spec] * 3 + [cache_spec] * 2
                     + [_const_spec(a.shape) for a in tiles])
        operands += small + [sample["q"], sample["k_new"], sample["v_new"], ck, cv] + tiles
        out_specs.append(new_spec)
        out_shape.append(jax.ShapeDtypeStruct((b_s * t_s, D_ATT), BF16))
        scratch = [pltpu.VMEM((N_HEADS, 2 * t_s, D_V), BF16), pltpu.VMEM((N_HEADS, 2 * t_s, 1), F32),
                   pltpu.VMEM((N_HEADS, 2 * t_s, 2 * D_V), F32), pltpu.VMEM((N_HEADS, 2 * t_s, tc), F32)]
        sample_cfg = dict(chunks_per_batch=per_batch, n_varying=sample["n_varying"],
                          lam_init=sample["lam_init"])
    outs = pl.pallas_call(
        functools.partial(_finish_kernel, alpha=alpha, sample_cfg=sample_cfg),
        grid=(nt,),
        in_specs=in_specs,
        out_specs=out_specs,
        out_shape=out_shape,
        scratch_shapes=scratch,
        compiler_params=pltpu.CompilerParams(
            dimension_semantics=("arbitrary",),
            vmem_limit_bytes=FUSED_VMEM_LIMIT if sample is not None else VMEM_LIMIT),
        name=name,
    )(*operands)
    return outs if sample is not None else outs[0]


def kernel(x_prompt, x_sample, cache_k, cache_v, rel_bias_table, w_in, lambda_q1, lambda_k1, lambda_q2,
           lambda_k2, subln_gain, gmlp_ln_gain, gmlp_ln_bias, w_spatial, b_spatial, w_out, ln1_gain,
           ln1_bias, w_ffn_up, w_ffn_down, ln2_gain, ln2_bias):
    depth = w_in.shape[0]
    assert depth == 1, "single-layer step"
    b_p, t_p, d_model = x_prompt.shape
    b_s, t_s, _ = x_sample.shape
    past = cache_k.shape[2]
    alpha = (2.0 * depth) ** 0.25
    lam_init = _lambda_init(0)
    t = ATT_TILE
    tc = CACHE_TILE
    assert t_p % t == 0 and past % tc == 0 and t % CHUNK == 0 and t_s == CHUNK and past % GMLP_CHUNK == 0
    assert ROW_TILE % t == 0 and (b_p * t_p) % ROW_TILE == 0 and (b_s * t_s) % ROW_TILE == 0

    bucket_table = _bucket_of_distance(past + t_s)
    n_prompt_bias = _num_varying_tiles(t, bucket_table) + 1
    n_sample_var = _num_varying_tiles(tc, bucket_table) - 1
    n_sample_bias = n_sample_var + 1
    assert past // tc > n_sample_var

    bias_p, bias_s, bias_n = pl.pallas_call(
        functools.partial(_bias_kernel, n_prompt=n_prompt_bias, t=t, n_sample=n_sample_bias, tc=tc,
                          past=past, ts=t_s, bucket_table=bucket_table),
        in_specs=[pl.BlockSpec(memory_space=pltpu.SMEM)],
        out_shape=[jax.ShapeDtypeStruct((N_HEADS, n_prompt_bias, t, t), F32),
                   jax.ShapeDtypeStruct((N_HEADS, n_sample_bias, t_s, tc), F32),
                   jax.ShapeDtypeStruct((N_HEADS, t_s, t_s), F32)],
        compiler_params=pltpu.CompilerParams(vmem_limit_bytes=VMEM_LIMIT),
        name="rel_bias_tiles",
    )(rel_bias_table)

    w_in_b = w_in[0].astype(BF16)
    w_out_b = w_out[0].astype(BF16)
    w_up_b = w_ffn_up[0].astype(BF16)
    w_down_b = w_ffn_down[0].astype(BF16)
    lng = gmlp_ln_gain[0].reshape(1, D_GMLP)
    lnb = gmlp_ln_bias[0].reshape(1, D_GMLP)
    bsp = jnp.repeat(b_spatial[0].T, GROUP_DIM, axis=1)
    g1, b1 = ln1_gain[0].reshape(1, d_model), ln1_bias[0].reshape(1, d_model)
    g2, b2 = ln2_gain[0].reshape(1, d_model), ln2_bias[0].reshape(1, d_model)
    lamv = jnp.stack([lambda_q1[0], lambda_k1[0], lambda_q2[0], lambda_k2[0]], axis=0)
    gain = subln_gain[0]

    xp = x_prompt.reshape(b_p * t_p, d_model)
    q_p, k_p, v_p, kb_p, vt_p, gm_p = _project(xp, w_in_b, lng, lnb, w_spatial[0], bsp,
                                               chunk=GMLP_CHUNK, att_tile=t, sample=False)
    nq = t_p // t
    gain_t = gain.T
    attn_p = pl.pallas_call(
        functools.partial(_attn_prompt_kernel, t=t, n_bias=n_prompt_bias, lam_init=lam_init),
        grid=(nq, b_p + 1),
        in_specs=[_const_spec(lamv.shape), _const_spec(gain_t.shape),
                  pl.BlockSpec((t, D_ATT), lambda i, bb: (jnp.minimum(bb, b_p - 1) * nq + i, 0)),
                  pl.BlockSpec((t_p, D_ATT), lambda i, bb: (jnp.minimum(bb, b_p - 1), 0)),
                  pl.BlockSpec((nq, N_HEADS * VT_ROWS, t), lambda i, bb: (jnp.maximum(bb - 1, 0), 0, 0)),
                  _const_spec(bias_p.shape)],
        out_specs=pl.BlockSpec((t, D_ATT), lambda i, bb: (jnp.maximum(bb - 1, 0) * nq + i, 0)),
        out_shape=jax.ShapeDtypeStruct((b_p * t_p, D_ATT), BF16),
        scratch_shapes=[pltpu.VMEM((N_HEADS, 2 * t, D_V), BF16), pltpu.VMEM((N_HEADS, VT_ROWS, 2 * t), F32),
                        pltpu.VMEM((N_HEADS, 1, 2 * t), F32), pltpu.VMEM((N_HEADS, 1, 2 * t), F32),
                        pltpu.VMEM((N_HEADS, t_p, 2 * t), F32), pltpu.VMEM((N_HEADS, t_p, 2 * t), F32)],
        compiler_params=_params("arbitrary", "arbitrary"),
        name="attn_prompt",
    )(lamv, gain_t, q_p, kb_p, vt_p, bias_p)

    xs = x_sample.reshape(b_s * t_s, d_model)
    q_s, k_s, v_s, kb_s, vb_s, gm_s, gn_s = _project(xs, w_in_b, lng, lnb, w_spatial[0], bsp[:t_s],
                                                     chunk=t_s, att_tile=t, sample=True)
    sample = dict(lamv=lamv, gain=gain, q=q_s, k_new=kb_s, v_new=vb_s,
                  cache_k=cache_k.reshape(b_s, past * N_HEADS, D_V),
                  cache_v=cache_v.reshape(b_s, past * N_HEADS, D_V), cache_tile=tc,
                  bias_cache=bias_s, bias_new=bias_n, n_varying=n_sample_var, lam_init=lam_init)
    y_p, attn_s = _finish(xp, attn_p, gm_p, w_out_b, g1, b1, w_up_b, w_down_b, g2, b2, alpha=alpha,
                          name="finish_prompt_attn_sample", sample=sample)
    y_s = _finish(xs, attn_s, gm_s, w_out_b, g1, b1, w_up_b, w_down_b, g2, b2, alpha=alpha,
                  name="finish_sample")

    head_shape_p = (depth, b_p, t_p, N_HEADS, D_V)
    head_shape_s = (depth, b_s, t_s, N_HEADS, D_V)
    return (y_p.reshape(b_p, t_p, d_model), y_s.reshape(b_s, t_s, d_model),
            k_p.reshape(head_shape_p), v_p.reshape(head_shape_p),
            k_s.reshape(head_shape_s), v_s.reshape(head_shape_s),
            gn_s.reshape(depth, b_s, t_s, N_GROUPS, GROUP_DIM))
```

```python
import functools
import math

import numpy as np
import jax
import jax.numpy as jnp
from jax import lax
from jax.experimental import pallas as pl
from jax.experimental.pallas import tpu as pltpu

F32 = jnp.float32
BF16 = jnp.bfloat16

CHUNK = 64
N_HEADS = 4
D_QK = 64
D_V = 2 * D_QK
D_ATT = N_HEADS * D_V
N_GROUPS = 4
GROUP_DIM = 128
D_GMLP = N_GROUPS * GROUP_DIM
GMLP_CHUNK = 128
NUM_BUCKETS = 32
MAX_DISTANCE = 1024
LN_EPS = 1e-5
NEG_INF = -1e30
QK_SCALE = D_QK ** -0.5
LOG2E = math.log2(math.e)
ONES_ROWS = 16
VT_ROWS = D_V + ONES_ROWS

ROW_TILE = 1024
SAMPLE_ROW_TILE = 512
FINISH_ROW_TILE = 512
FINISH_PART_QUARTERS = (2,)
ATT_TILE = 256
KEY_TILES_PER_TRIP = 2
CACHE_TILE = 2048
FFN_CHUNK = 1024
VMEM_LIMIT = 56 * 1024 * 1024
FUSED_VMEM_LIMIT = 60 * 1024 * 1024

_NT = (((1,), (1,)), ((), ()))


def _lambda_init(layer):
    return 0.8 - 0.6 * math.exp(-0.3 * layer)


def _bucket_of_distance(max_n):
    half = NUM_BUCKETS // 2
    max_exact = half // 2
    n = np.arange(max_n + 1)

    def table(dtype):
        nf = np.maximum(n, 1).astype(dtype)
        scaled = np.log(nf / dtype(max_exact)) / dtype(math.log(MAX_DISTANCE / max_exact)) * dtype(half - max_exact)
        large = np.minimum(max_exact + scaled.astype(np.int64), half - 1)
        return np.where(n < max_exact, n, large)

    t64 = table(np.float64)
    assert np.array_equal(t64, table(np.float32)), "bucket boundaries are precision sensitive"
    return t64


def _rel_bucket_static(rel, bucket_table):
    return (NUM_BUCKETS // 2 if rel > 0 else 0) + int(bucket_table[abs(rel)])


def _bias_tile(tab_ref, head, d0, nrows, ncols, bucket_table, *, keys_on_rows=False):
    row = lax.broadcasted_iota(jnp.int32, (nrows, ncols), 0)
    col = lax.broadcasted_iota(jnp.int32, (nrows, ncols), 1)
    if keys_on_rows:
        d = (row - col) + d0
        dmin, dmax = d0 - (ncols - 1), d0 + nrows - 1
    else:
        d = (col - row) + d0
        dmin, dmax = d0 - (nrows - 1), d0 + ncols - 1
    prev = _rel_bucket_static(dmin, bucket_table)
    val = jnp.full((nrows, ncols), tab_ref[prev, head] * LOG2E, F32)
    for dd in range(dmin + 1, dmax + 1):
        b = _rel_bucket_static(dd, bucket_table)
        if b != prev:
            val = jnp.where(d >= dd, tab_ref[b, head] * LOG2E, val)
            prev = b
    return val


def _bias_kernel(tab_ref, bp_ref, bs_ref, bn_ref, *, n_prompt, t, n_sample, tc, past, ts, bucket_table):
    key = lax.broadcasted_iota(jnp.int32, (t, t), 0)
    qry = lax.broadcasted_iota(jnp.int32, (t, t), 1)
    visible = (key // CHUNK) <= (qry // CHUNK)
    n_cache_tiles = past // tc
    for h in range(N_HEADS):
        for dl in range(n_prompt):
            tile = _bias_tile(tab_ref, h, -dl * t, t, t, bucket_table, keys_on_rows=True)
            if dl == 0:
                tile = jnp.where(visible, tile, NEG_INF)
            bp_ref[h, dl] = tile
        for i in range(n_sample):
            j = n_cache_tiles - n_sample + i
            bs_ref[h, i] = _bias_tile(tab_ref, h, j * tc - past, ts, tc, bucket_table)
        bn_ref[h] = _bias_tile(tab_ref, h, 0, ts, ts, bucket_table)


def _far_distance(bucket_table):
    n = len(bucket_table) - 1
    while n > 0 and bucket_table[n - 1] == bucket_table[-1]:
        n -= 1
    return n


def _num_varying_tiles(tile, bucket_table):
    far = _far_distance(bucket_table)
    n = 1
    while (n - 1) * tile + 1 < far:
        n += 1
    return n


def _gelu_tanh(x):
    c = math.sqrt(2.0 / math.pi)
    return 0.5 * x * (1.0 + jnp.tanh(c * (x + 0.044715 * (x * x * x))))


def _layer_norm(x, g, b):
    mu = jnp.mean(x, axis=-1, keepdims=True)
    xc = x - mu
    var = jnp.mean(xc * xc, axis=-1, keepdims=True)
    return xc * lax.rsqrt(var + LN_EPS) * g + b


def _store_heads(ref, x, n):
    rows = x.shape[0]
    for h in range(n):
        ref[pl.ds(h, rows, stride=n), :] = x[:, h * 128:(h + 1) * 128]


def _proj_kernel(x_ref, w_ref, lng_ref, lnb_ref, ws_ref, bsp_ref, *refs, chunk, att_tile, sample, n_cast):
    for src, dst in zip(refs[:n_cast], refs[len(refs) - n_cast:]):
        dst[...] = src[...].astype(BF16)
    out_refs = refs[n_cast:len(refs) - n_cast]
    if sample:
        q_ref, kf_ref, vf_ref, kb_ref, vb_ref, gm_ref, gn_ref = out_refs
    else:
        q_ref, kf_ref, vf_ref, kb_ref, vt_ref, gm_ref = out_refs
    tm = x_ref.shape[0]
    xb = x_ref[...].astype(BF16)

    def seg(lo, hi):
        return jnp.dot(xb, w_ref[:, lo:hi], preferred_element_type=F32)

    g_lin = seg(3 * D_ATT + D_GMLP, 3 * D_ATT + 2 * D_GMLP)
    u_lin = seg(3 * D_ATT, 3 * D_ATT + D_GMLP)

    lng = lng_ref[...]
    lnb = lnb_ref[...]
    g = _gelu_tanh(g_lin)
    gn = jnp.concatenate(
        [_layer_norm(g[:, i * GROUP_DIM:(i + 1) * GROUP_DIM],
                     lng[:, i * GROUP_DIM:(i + 1) * GROUP_DIM],
                     lnb[:, i * GROUP_DIM:(i + 1) * GROUP_DIM]) for i in range(N_GROUPS)], axis=1)
    if sample:
        _store_heads(gn_ref, gn, N_GROUPS)
    gnb = gn.astype(BF16)

    v = seg(2 * D_ATT, 3 * D_ATT)
    _store_heads(vf_ref, v, N_HEADS)
    if sample:
        vb_ref[...] = v.astype(BF16)
    else:
        ones = jnp.ones((ONES_ROWS, att_tile), F32)
        for j in range(tm // att_tile):
            vt = v[j * att_tile:(j + 1) * att_tile, :].T
            vt_ref[j] = jnp.concatenate(
                [part for h in range(N_HEADS) for part in (vt[h * D_V:(h + 1) * D_V], ones)],
                axis=0).astype(BF16)

    u = _gelu_tanh(u_lin)

    k = seg(D_ATT, 2 * D_ATT)
    _store_heads(kf_ref, k, N_HEADS)
    kb_ref[...] = k.astype(BF16)
    q_ref[...] = (seg(0, D_ATT) * (QK_SCALE * LOG2E)).astype(BF16)

    ri = lax.broadcasted_iota(jnp.int32, (chunk, chunk), 0)
    ci = lax.broadcasted_iota(jnp.int32, (chunk, chunk), 1)
    keep = (ri // CHUNK) >= (ci // CHUNK)
    ws = [jnp.where(keep, ws_ref[i, :chunk, :chunk], 0.0).astype(BF16) for i in range(N_GROUPS)]
    bsp = bsp_ref[...]
    for c in range(tm // chunk):
        r0 = c * chunk
        sp = jnp.concatenate(
            [jnp.dot(ws[i], gnb[r0:r0 + chunk, i * GROUP_DIM:(i + 1) * GROUP_DIM],
                     preferred_element_type=F32) for i in range(N_GROUPS)], axis=1)
        gm_ref[r0:r0 + chunk, :] = (u[r0:r0 + chunk, :] * (sp + bsp)).astype(BF16)


def _diff_lambda(lamv_ref, lam_init):
    lv = lamv_ref[...]
    s1 = jnp.sum(lv[0:1, :] * lv[1:2, :], axis=-1, keepdims=True)
    s2 = jnp.sum(lv[2:3, :] * lv[3:4, :], axis=-1, keepdims=True)
    return jnp.exp(s1) - jnp.exp(s2) + lam_init


def _stack_halves(qh):
    lane = lax.broadcasted_iota(jnp.int32, qh.shape, 1)
    zero = jnp.zeros_like(qh)
    return jnp.concatenate([jnp.where(lane < D_QK, qh, zero), jnp.where(lane >= D_QK, qh, zero)], axis=0)


def _attn_prompt_kernel(lamv_ref, gaint_ref, q_ref, k_ref, vt_ref, bias_ref, o_ref,
                        qz_ref, acc_ref, m_even, m_odd, s_even, s_odd, *, t, n_bias, lam_init):
    qi = pl.program_id(0)
    bb = pl.program_id(1)
    nb = pl.num_programs(1) - 1
    n_tiles = qi + 1
    has1 = bb < nb
    has2 = bb >= 1

    def pass1_head(s_ref, m_ref, h, kj, width):
        off = pl.multiple_of(kj * t, t)
        s = lax.dot_general(k_ref[pl.ds(off, width * t), h * D_V:(h + 1) * D_V], qz_ref[h], _NT,
                            preferred_element_type=F32)
        tiles = [bias_ref[h, jnp.minimum(qi - (kj + i), n_bias - 1)] for i in range(width)]
        s = s + jnp.concatenate([jnp.concatenate([b, b], axis=1) for b in tiles], axis=0)
        s_ref[h, pl.ds(off, width * t), :] = s
        m_ref[h] = jnp.maximum(m_ref[h], jnp.max(s, axis=0, keepdims=True))

    def pass2_head(s_ref, m_ref, h, kj, width):
        off = pl.multiple_of(kj * t, t)
        p = jnp.exp2(s_ref[h, pl.ds(off, width * t), :] - m_ref[h])
        vt = jnp.concatenate([vt_ref[kj + i, h * VT_ROWS:(h + 1) * VT_ROWS, :] for i in range(width)],
                             axis=1)
        acc_ref[h] = acc_ref[h] + jnp.dot(vt, p.astype(BF16), preferred_element_type=F32)

    def over_key_tiles(per_head_steps):
        def tiles(kj, width):
            for h in range(N_HEADS):
                for step in per_head_steps:
                    step(h, kj, width)

        widest = KEY_TILES_PER_TRIP

        def trip(i, carry):
            tiles(widest * i, widest)
            return carry

        lax.fori_loop(0, n_tiles // widest, trip, 0)
        done = (n_tiles // widest) * widest
        width = widest // 2
        while width >= 1:
            @pl.when((n_tiles & width) != 0)
            def _(width=width, done=done):
                tiles(done, width)

            done = done + (n_tiles & width)
            width //= 2

    def step(s_new, m_new, s_old, m_old):
        p1 = functools.partial(pass1_head, s_new, m_new)
        p2 = functools.partial(pass2_head, s_old, m_old)

        @pl.when(has1)
        def _():
            for h in range(N_HEADS):
                qz_ref[h] = _stack_halves(q_ref[:, h * D_V:(h + 1) * D_V])
            m_new[...] = jnp.full(m_new.shape, -jnp.inf, F32)

        @pl.when(jnp.logical_and(has1, has2))
        def _():
            over_key_tiles((p1, p2))

        @pl.when(jnp.logical_not(has2))
        def _():
            over_key_tiles((p1,))

        @pl.when(jnp.logical_not(has1))
        def _():
            over_key_tiles((p2,))

    @pl.when(has2)
    def _():
        acc_ref[...] = jnp.zeros(acc_ref.shape, F32)

    @pl.when(jnp.logical_not(has2))
    def _():
        o_ref[...] = jnp.zeros(o_ref.shape, o_ref.dtype)

    @pl.when(bb % 2 == 0)
    def _():
        step(s_even, m_even, s_odd, m_odd)

    @pl.when(bb % 2 == 1)
    def _():
        step(s_odd, m_odd, s_even, m_even)

    @pl.when(has2)
    def _():
        lam = _diff_lambda(lamv_ref, lam_init)
        for h in range(N_HEADS):
            acc = acc_ref[h]
            o = acc[:D_V] / acc[D_V:D_V + 1]
            od = o[:, :t] - lam * o[:, t:]
            ms = jnp.mean(od * od, axis=0, keepdims=True)
            y = od * lax.rsqrt(ms + LN_EPS) * gaint_ref[:, h:h + 1] * (1.0 - lam_init)
            o_ref[:, h * D_V:(h + 1) * D_V] = y.T.astype(BF16)


def _sample_attention_chunk(chunk, n_chunks, lamv_ref, gain_ref, q_ref, kn_ref, vn_ref, kc_ref, vc_ref,
                            bias_ref, biasn_ref, o_ref, qz_ref, m_ref, acc_ref, s_ref, *, n_varying, lam_init):
    ts = q_ref.shape[0]
    tc = kc_ref.shape[0] // N_HEADS

    def raise_max(h, s):
        m_old = m_ref[h]
        m_new = jnp.maximum(m_old, jnp.max(s, axis=-1, keepdims=True))
        m_ref[h] = m_new
        acc_ref[h] = acc_ref[h] * jnp.exp2(m_old - m_new)
        return m_new

    def weighted_values(p, v):
        v1 = jnp.concatenate([v, jnp.ones(v.shape, BF16)], axis=1)
        return jnp.dot(p.astype(BF16), v1, preferred_element_type=F32)

    def begin():
        @pl.when(chunk == 0)
        def _():
            for h in range(N_HEADS):
                qz_ref[h] = _stack_halves(q_ref[:, h * D_V:(h + 1) * D_V])
            m_ref[...] = jnp.full(m_ref.shape, -jnp.inf, F32)
            acc_ref[...] = jnp.zeros(acc_ref.shape, F32)

    def scores(h):
        bi = jnp.maximum(chunk - (n_chunks - n_varying - 1), 0)
        kh = kc_ref[pl.ds(h, tc, stride=N_HEADS), :].astype(BF16)
        s = lax.dot_general(qz_ref[h], kh, _NT, preferred_element_type=F32)
        bias = bias_ref[h, bi]
        s = s + jnp.concatenate([bias, bias], axis=0)
        s_ref[h] = s
        raise_max(h, s)

    def values(h):
        p = jnp.exp2(s_ref[h] - m_ref[h])
        vh = vc_ref[pl.ds(h, tc, stride=N_HEADS), :].astype(BF16)
        acc_ref[h] = acc_ref[h] + weighted_values(p, vh)

    def finish():
        @pl.when(chunk == n_chunks - 1)
        def _():
            lam = _diff_lambda(lamv_ref, lam_init)
            heads = [slice(h * D_V, (h + 1) * D_V) for h in range(N_HEADS)]
            scores_new = []
            for h, hs in enumerate(heads):
                s = lax.dot_general(qz_ref[h], kn_ref[:, hs], _NT, preferred_element_type=F32)
                bias = biasn_ref[h]
                scores_new.append(s + jnp.concatenate([bias, bias], axis=0))
            accs = []
            for h, hs in enumerate(heads):
                m_new = raise_max(h, scores_new[h])
                accs.append(acc_ref[h] + weighted_values(jnp.exp2(scores_new[h] - m_new), vn_ref[:, hs]))
            for h, hs in enumerate(heads):
                acc = accs[h]
                o = acc[:, :D_V] / acc[:, D_V:D_V + 1]
                od = o[:ts] - lam * o[ts:]
                ms = jnp.mean(od * od, axis=-1, keepdims=True)
                y = od * lax.rsqrt(ms + LN_EPS) * gain_ref[h:h + 1, :] * (1.0 - lam_init)
                o_ref[:, hs] = y.astype(BF16)

    per_head = lambda fn: [functools.partial(fn, h) for h in range(N_HEADS)]
    return begin, per_head(scores), per_head(values), finish


_N_SAMPLE_INPUTS = 9


def _ffn_up(x1b, wu_ref, c):
    hc = jnp.dot(x1b, wu_ref[:, c * FFN_CHUNK:(c + 1) * FFN_CHUNK], preferred_element_type=F32)
    return jnp.square(jnp.maximum(hc, 0.0)).astype(BF16)


def _ffn_down(hc, wd_ref, c):
    return jnp.dot(hc, wd_ref[c * FFN_CHUNK:(c + 1) * FFN_CHUNK, :], preferred_element_type=F32)


def _finish_kernel(x_ref, a_ref, gm_ref, wo_ref, g1_ref, b1_ref, wu_ref, wd_ref, g2_ref, b2_ref, *rest,
                   alpha, sample_cfg):
    between = {}
    sample_finish = None
    if sample_cfg is None:
        (y_ref,) = rest
    else:
        y_ref = rest[_N_SAMPLE_INPUTS]
        per_batch = sample_cfg["chunks_per_batch"]
        sample_begin, between[0], between[1], sample_finish = _sample_attention_chunk(
            pl.program_id(0) % per_batch, per_batch, *rest[:_N_SAMPLE_INPUTS], *rest[_N_SAMPLE_INPUTS + 1:],
            n_varying=sample_cfg["n_varying"], lam_init=sample_cfg["lam_init"])
        sample_begin()

    tm = x_ref.shape[0]
    bounds = [0] + [tm * f // 4 for f in FINISH_PART_QUARTERS] + [tm]
    parts = [slice(lo, hi) for lo, hi in zip(bounds[:-1], bounds[1:])]

    def out_proj_ln(rs):
        mixed = jnp.concatenate([a_ref[rs, :], gm_ref[rs, :]], axis=1)
        r = jnp.dot(mixed, wo_ref[...], preferred_element_type=F32)
        return _layer_norm(alpha * x_ref[rs, :] + r, g1_ref[...], b1_ref[...])

    def ffn(x1, between=()):
        x1b = x1.astype(BF16)
        f = alpha * x1
        for c in range(wu_ref.shape[1] // FFN_CHUNK):
            f = f + _ffn_down(_ffn_up(x1b, wu_ref, c), wd_ref, c)
            if c < len(between):
                between[c]()
        return f

    x1 = out_proj_ln(parts[0])
    for j, rs in enumerate(parts):
        x1_next = out_proj_ln(parts[j + 1]) if j + 1 < len(parts) else None
        y_ref[rs, :] = _layer_norm(ffn(x1, between.get(j, ())), g2_ref[...], b2_ref[...])
        x1 = x1_next
    if sample_finish is not None:
        sample_finish()


def _const_spec(shape):
    nd = len(shape)
    return pl.BlockSpec(shape, lambda *_: (0,) * nd, pipeline_mode=pl.Buffered(1))


def _params(*sem):
    return pltpu.CompilerParams(dimension_semantics=sem, vmem_limit_bytes=VMEM_LIMIT)


def _project(x2d, w_in, lng, lnb, w_spatial, bsp, *, chunk, att_tile, sample, cast_along=()):
    n, d = x2d.shape
    tm = SAMPLE_ROW_TILE if sample else ROW_TILE
    steps = n // tm
    row = lambda width: pl.BlockSpec((tm, width), lambda i: (i, 0))
    per_head = pl.BlockSpec((tm * N_HEADS, D_V), lambda i: (i, 0))
    out_shape = [jax.ShapeDtypeStruct((n, D_ATT), BF16), jax.ShapeDtypeStruct((n * N_HEADS, D_V), F32),
                 jax.ShapeDtypeStruct((n * N_HEADS, D_V), F32), jax.ShapeDtypeStruct((n, D_ATT), BF16)]
    out_specs = [row(D_ATT), per_head, per_head, row(D_ATT)]
    if sample:
        out_shape += [jax.ShapeDtypeStruct((n, D_ATT), BF16), jax.ShapeDtypeStruct((n, D_GMLP), BF16),
                      jax.ShapeDtypeStruct((n * N_GROUPS, GROUP_DIM), F32)]
        out_specs += [row(D_ATT), row(D_GMLP), pl.BlockSpec((tm * N_GROUPS, GROUP_DIM), lambda i: (i, 0))]
    else:
        out_shape += [jax.ShapeDtypeStruct((n // att_tile, N_HEADS * VT_ROWS, att_tile), BF16),
                      jax.ShapeDtypeStruct((n, D_GMLP), BF16)]
        out_specs += [pl.BlockSpec((tm // att_tile, N_HEADS * VT_ROWS, att_tile), lambda i: (i, 0, 0)), row(D_GMLP)]
    cast_specs = [pl.BlockSpec((w.shape[0] // steps, w.shape[1]), lambda i: (i, 0)) for w in cast_along]
    out_shape += [jax.ShapeDtypeStruct(w.shape, BF16) for w in cast_along]
    return pl.pallas_call(
        functools.partial(_proj_kernel, chunk=chunk, att_tile=att_tile, sample=sample, n_cast=len(cast_along)),
        grid=(steps,),
        in_specs=[row(d), _const_spec(w_in.shape), _const_spec(lng.shape), _const_spec(lnb.shape),
                  _const_spec(w_spatial.shape), _const_spec(bsp.shape)] + cast_specs,
        out_specs=out_specs + cast_specs,
        out_shape=out_shape,
        compiler_params=_params("parallel"),
        name="proj_sample" if sample else "proj_prompt",
    )(x2d, w_in, lng, lnb, w_spatial, bsp, *cast_along)


def _finish(x2d, attn, gm, w_out, g1, b1, w_up, w_down, g2, b2, *, alpha, name, sample=None):
    n, d = x2d.shape
    tm = min(FINISH_ROW_TILE, n)
    nt = n // tm
    row = lambda width: pl.BlockSpec((tm, width), lambda i: (i, 0))
    in_specs = [row(d), row(D_ATT), row(D_GMLP), _const_spec(w_out.shape), _const_spec(g1.shape),
                _const_spec(b1.shape), _const_spec(w_up.shape), _const_spec(w_down.shape),
                _const_spec(g2.shape), _const_spec(b2.shape)]
    operands = [x2d, attn, gm, w_out, g1, b1, w_up, w_down, g2, b2]
    out_specs = [row(d)]
    out_shape = [jax.ShapeDtypeStruct((n, d), F32)]
    scratch = []
    sample_cfg = None
    if sample is not None:
        ck, cv = sample["cache_k"], sample["cache_v"]
        b_s, tc = ck.shape[0], sample["cache_tile"]
        per_batch = ck.shape[1] // (tc * N_HEADS)
        assert nt == b_s * per_batch, "one cache chunk per grid step"
        t_s = sample["q"].shape[0] // b_s
        new_spec = pl.BlockSpec((t_s, D_ATT), lambda i: (i // per_batch, 0))
        cache_spec = pl.BlockSpec((None, tc * N_HEADS, D_V), lambda i: (i // per_batch, i % per_batch, 0))
        small = [sample["lamv"], sample["gain"]]
        tiles = [sample["bias_cache"], sample["bias_new"]]
        in_specs += ([_const_spec(a.shape) for a in small] + [new_spec] * 3 + [cache_spec] * 2
                     + [_const_spec(a.shape) for a in tiles])
        operands += small + [sample["q"], sample["k_new"], sample["v_new"], ck, cv] + tiles
        out_specs.append(new_spec)
        out_shape.append(jax.ShapeDtypeStruct((b_s * t_s, D_ATT), BF16))
        scratch = [pltpu.VMEM((N_HEADS, 2 * t_s, D_V), BF16), pltpu.VMEM((N_HEADS, 2 * t_s, 1), F32),
                   pltpu.VMEM((N_HEADS, 2 * t_s, 2 * D_V), F32), pltpu.VMEM((N_HEADS, 2 * t_s, tc), F32)]
        sample_cfg = dict(chunks_per_batch=per_batch, n_varying=sample["n_varying"],
                          lam_init=sample["lam_init"])
    outs = pl.pallas_call(
        functools.partial(_finish_kernel, alpha=alpha, sample_cfg=sample_cfg),
        grid=(nt,),
        in_specs=in_specs,
        out_specs=out_specs,
        out_shape=out_shape,
        scratch_shapes=scratch,
        compiler_params=pltpu.CompilerParams(
            dimension_semantics=("arbitrary",),
            vmem_limit_bytes=FUSED_VMEM_LIMIT if sample is not None else VMEM_LIMIT),
        name=name,
    )(*operands)
    return outs if sample is not None else outs[0]


def kernel(x_prompt, x_sample, cache_k, cache_v, rel_bias_table, w_in, lambda_q1, lambda_k1, lambda_q2,
           lambda_k2, subln_gain, gmlp_ln_gain, gmlp_ln_bias, w_spatial, b_spatial, w_out, ln1_gain,
           ln1_bias, w_ffn_up, w_ffn_down, ln2_gain, ln2_bias):
    depth = w_in.shape[0]
    assert depth == 1, "single-layer step"
    b_p, t_p, d_model = x_prompt.shape
    b_s, t_s, _ = x_sample.shape
    past = cache_k.shape[2]
    alpha = (2.0 * depth) ** 0.25
    lam_init = _lambda_init(0)
    t = ATT_TILE
    tc = CACHE_TILE
    assert t_p % t == 0 and past % tc == 0 and t % CHUNK == 0 and t_s == CHUNK and past % GMLP_CHUNK == 0
    assert ROW_TILE % t == 0 and (b_p * t_p) % ROW_TILE == 0 and (b_s * t_s) % SAMPLE_ROW_TILE == 0
    assert SAMPLE_ROW_TILE % t_s == 0

    bucket_table = _bucket_of_distance(past + t_s)
    n_prompt_bias = _num_varying_tiles(t, bucket_table) + 1
    n_sample_var = _num_varying_tiles(tc, bucket_table) - 1
    n_sample_bias = n_sample_var + 1
    assert past // tc > n_sample_var

    bias_p, bias_s, bias_n = pl.pallas_call(
        functools.partial(_bias_kernel, n_prompt=n_prompt_bias, t=t, n_sample=n_sample_bias, tc=tc,
                          past=past, ts=t_s, bucket_table=bucket_table),
        in_specs=[pl.BlockSpec(memory_space=pltpu.SMEM)],
        out_shape=[jax.ShapeDtypeStruct((N_HEADS, n_prompt_bias, t, t), F32),
                   jax.ShapeDtypeStruct((N_HEADS, n_sample_bias, t_s, tc), F32),
                   jax.ShapeDtypeStruct((N_HEADS, t_s, t_s), F32)],
        compiler_params=pltpu.CompilerParams(vmem_limit_bytes=VMEM_LIMIT),
        name="rel_bias_tiles",
    )(rel_bias_table)

    w_in_b = w_in[0].astype(BF16)
    lng = gmlp_ln_gain[0].reshape(1, D_GMLP)
    lnb = gmlp_ln_bias[0].reshape(1, D_GMLP)
    bsp = jnp.repeat(b_spatial[0].T, GROUP_DIM, axis=1)
    g1, b1 = ln1_gain[0].reshape(1, d_model), ln1_bias[0].reshape(1, d_model)
    g2, b2 = ln2_gain[0].reshape(1, d_model), ln2_bias[0].reshape(1, d_model)
    lamv = jnp.stack([lambda_q1[0], lambda_k1[0], lambda_q2[0], lambda_k2[0]], axis=0)
    gain = subln_gain[0]

    xp = x_prompt.reshape(b_p * t_p, d_model)
    (q_p, k_p, v_p, kb_p, vt_p, gm_p, w_out_b, w_up_b, w_down_b) = _project(
        xp, w_in_b, lng, lnb, w_spatial[0], bsp, chunk=GMLP_CHUNK, att_tile=t, sample=False,
        cast_along=(w_out[0], w_ffn_up[0], w_ffn_down[0]))
    nq = t_p // t
    gain_t = gain.T
    attn_p = pl.pallas_call(
        functools.partial(_attn_prompt_kernel, t=t, n_bias=n_prompt_bias, lam_init=lam_init),
        grid=(nq, b_p + 1),
        in_specs=[_const_spec(lamv.shape), _const_spec(gain_t.shape),
                  pl.BlockSpec((t, D_ATT), lambda i, bb: (jnp.minimum(bb, b_p - 1) * nq + i, 0)),
                  pl.BlockSpec((t_p, D_ATT), lambda i, bb: (jnp.minimum(bb, b_p - 1), 0)),
                  pl.BlockSpec((nq, N_HEADS * VT_ROWS, t), lambda i, bb: (jnp.maximum(bb - 1, 0), 0, 0)),
                  _const_spec(bias_p.shape)],
        out_specs=pl.BlockSpec((t, D_ATT), lambda i, bb: (jnp.maximum(bb - 1, 0) * nq + i, 0)),
        out_shape=jax.ShapeDtypeStruct((b_p * t_p, D_ATT), BF16),
        scratch_shapes=[pltpu.VMEM((N_HEADS, 2 * t, D_V), BF16), pltpu.VMEM((N_HEADS, VT_ROWS, 2 * t), F32),
                        pltpu.VMEM((N_HEADS, 1, 2 * t), F32), pltpu.VMEM((N_HEADS, 1, 2 * t), F32),
                        pltpu.VMEM((N_HEADS, t_p, 2 * t), F32), pltpu.VMEM((N_HEADS, t_p, 2 * t), F32)],
        compiler_params=_params("arbitrary", "arbitrary"),
        name="attn_prompt",
    )(lamv, gain_t, q_p, kb_p, vt_p, bias_p)

    xs = x_sample.reshape(b_s * t_s, d_model)
    q_s, k_s, v_s, kb_s, vb_s, gm_s, gn_s = _project(xs, w_in_b, lng, lnb, w_spatial[0], bsp[:t_s],
                                                     chunk=t_s, att_tile=t, sample=True)
    sample = dict(lamv=lamv, gain=gain, q=q_s, k_new=kb_s, v_new=vb_s,
                  cache_k=cache_k.reshape(b_s, past * N_HEADS, D_V),
                  cache_v=cache_v.reshape(b_s, past * N_HEADS, D_V), cache_tile=tc,
                  bias_cache=bias_s, bias_new=bias_n, n_varying=n_sample_var, lam_init=lam_init)
    y_p, attn_s = _finish(xp, attn_p, gm_p, w_out_b, g1, b1, w_up_b, w_down_b, g2, b2, alpha=alpha,
                          name="finish_prompt_attn_sample", sample=sample)
    y_s = _finish(xs, attn_s, gm_s, w_out_b, g1, b1, w_up_b, w_down_b, g2, b2, alpha=alpha,
                  name="finish_sample")

    head_shape_p = (depth, b_p, t_p, N_HEADS, D_V)
    head_shape_s = (depth, b_s, t_s, N_HEADS, D_V)
    return (y_p.reshape(b_p, t_p, d_model), y_s.reshape(b_s, t_s, d_model),
            k_p.reshape(head_shape_p), v_p.reshape(head_shape_p),
            k_s.reshape(head_shape_s), v_s.reshape(head_shape_s),
            gn_s.reshape(depth, b_s, t_s, N_GROUPS, GROUP_DIM))
```

```python
import functools
import math

import numpy as np
import jax
import jax.numpy as jnp
from jax import lax
from jax.experimental import pallas as pl
from jax.experimental.pallas import tpu as pltpu

F32 = jnp.float32
BF16 = jnp.bfloat16

CHUNK = 64
N_HEADS = 4
D_QK = 64
D_V = 2 * D_QK
D_ATT = N_HEADS * D_V
N_GROUPS = 4
GROUP_DIM = 128
D_GMLP = N_GROUPS * GROUP_DIM
GMLP_CHUNK = 128
NUM_BUCKETS = 32
MAX_DISTANCE = 1024
LN_EPS = 1e-5
NEG_INF = -1e30
QK_SCALE = D_QK ** -0.5
LOG2E = math.log2(math.e)
ONES_ROWS = 16
VT_ROWS = D_V + ONES_ROWS

ROW_TILE = 1024
SAMPLE_ROW_TILE = 512
FINISH_ROW_TILE = 512
FINISH_PART_QUARTERS = (2,)
ATT_TILE = 256
KEY_TILES_PER_TRIP = 2
CACHE_TILE = 2048
FFN_CHUNK = 1024
VMEM_LIMIT = 56 * 1024 * 1024
FUSED_VMEM_LIMIT = 60 * 1024 * 1024

_NT = (((1,), (1,)), ((), ()))


def _lambda_init(layer):
    return 0.8 - 0.6 * math.exp(-0.3 * layer)


def _bucket_of_distance(max_n):
    half = NUM_BUCKETS // 2
    max_exact = half // 2
    n = np.arange(max_n + 1)

    def table(dtype):
        nf = np.maximum(n, 1).astype(dtype)
        scaled = np.log(nf / dtype(max_exact)) / dtype(math.log(MAX_DISTANCE / max_exact)) * dtype(half - max_exact)
        large = np.minimum(max_exact + scaled.astype(np.int64), half - 1)
        return np.where(n < max_exact, n, large)

    t64 = table(np.float64)
    assert np.array_equal(t64, table(np.float32)), "bucket boundaries are precision sensitive"
    return t64


def _rel_bucket_static(rel, bucket_table):
    return (NUM_BUCKETS // 2 if rel > 0 else 0) + int(bucket_table[abs(rel)])


def _bias_tile(tab_ref, head, d0, nrows, ncols, bucket_table, *, keys_on_rows=False):
    row = lax.broadcasted_iota(jnp.int32, (nrows, ncols), 0)
    col = lax.broadcasted_iota(jnp.int32, (nrows, ncols), 1)
    if keys_on_rows:
        d = (row - col) + d0
        dmin, dmax = d0 - (ncols - 1), d0 + nrows - 1
    else:
        d = (col - row) + d0
        dmin, dmax = d0 - (nrows - 1), d0 + ncols - 1
    prev = _rel_bucket_static(dmin, bucket_table)
    val = jnp.full((nrows, ncols), tab_ref[prev, head] * LOG2E, F32)
    for dd in range(dmin + 1, dmax + 1):
        b = _rel_bucket_static(dd, bucket_table)
        if b != prev:
            val = jnp.where(d >= dd, tab_ref[b, head] * LOG2E, val)
            prev = b
    return val


def _bias_kernel(tab_ref, lamv_ref, bp_ref, bs_ref, bn_ref, lam_ref, *, n_prompt, t, n_sample, tc, past, ts,
                 bucket_table, lam_init):
    lam_ref[...] = jnp.broadcast_to(_diff_lambda(lamv_ref, lam_init), lam_ref.shape)
    key = lax.broadcasted_iota(jnp.int32, (t, t), 0)
    qry = lax.broadcasted_iota(jnp.int32, (t, t), 1)
    visible = (key // CHUNK) <= (qry // CHUNK)
    n_cache_tiles = past // tc
    for h in range(N_HEADS):
        for dl in range(n_prompt):
            tile = _bias_tile(tab_ref, h, -dl * t, t, t, bucket_table, keys_on_rows=True)
            if dl == 0:
                tile = jnp.where(visible, tile, NEG_INF)
            bp_ref[h, dl] = tile
        for i in range(n_sample):
            j = n_cache_tiles - n_sample + i
            bs_ref[h, i] = _bias_tile(tab_ref, h, j * tc - past, ts, tc, bucket_table)
        bn_ref[h] = _bias_tile(tab_ref, h, 0, ts, ts, bucket_table)


def _far_distance(bucket_table):
    n = len(bucket_table) - 1
    while n > 0 and bucket_table[n - 1] == bucket_table[-1]:
        n -= 1
    return n


def _num_varying_tiles(tile, bucket_table):
    far = _far_distance(bucket_table)
    n = 1
    while (n - 1) * tile + 1 < far:
        n += 1
    return n


def _gelu_tanh(x):
    c = math.sqrt(2.0 / math.pi)
    return 0.5 * x * (1.0 + jnp.tanh(c * (x + 0.044715 * (x * x * x))))


def _layer_norm(x, g, b):
    mu = jnp.mean(x, axis=-1, keepdims=True)
    xc = x - mu
    var = jnp.mean(xc * xc, axis=-1, keepdims=True)
    return xc * lax.rsqrt(var + LN_EPS) * g + b


def _store_heads(ref, x, n):
    rows = x.shape[0]
    for h in range(n):
        ref[pl.ds(h, rows, stride=n), :] = x[:, h * 128:(h + 1) * 128]


def _proj_kernel(x_ref, w_ref, lng_ref, lnb_ref, ws_ref, bsp_ref, *refs, chunk, att_tile, sample, n_cast):
    for src, dst in zip(refs[:n_cast], refs[len(refs) - n_cast:]):
        dst[...] = src[...].astype(BF16)
    out_refs = refs[n_cast:len(refs) - n_cast]
    if sample:
        q_ref, kf_ref, vf_ref, kb_ref, vb_ref, gm_ref, gn_ref = out_refs
    else:
        q_ref, kf_ref, vf_ref, kb_ref, vt_ref, gm_ref = out_refs
    tm = x_ref.shape[0]
    xb = x_ref[...].astype(BF16)

    def seg(lo, hi):
        return jnp.dot(xb, w_ref[:, lo:hi], preferred_element_type=F32)

    g_lin = seg(3 * D_ATT + D_GMLP, 3 * D_ATT + 2 * D_GMLP)
    u_lin = seg(3 * D_ATT, 3 * D_ATT + D_GMLP)

    lng = lng_ref[...]
    lnb = lnb_ref[...]
    g = _gelu_tanh(g_lin)
    gn = jnp.concatenate(
        [_layer_norm(g[:, i * GROUP_DIM:(i + 1) * GROUP_DIM],
                     lng[:, i * GROUP_DIM:(i + 1) * GROUP_DIM],
                     lnb[:, i * GROUP_DIM:(i + 1) * GROUP_DIM]) for i in range(N_GROUPS)], axis=1)
    if sample:
        _store_heads(gn_ref, gn, N_GROUPS)
    gnb = gn.astype(BF16)

    v = seg(2 * D_ATT, 3 * D_ATT)
    _store_heads(vf_ref, v, N_HEADS)
    if sample:
        vb_ref[...] = v.astype(BF16)
    else:
        ones = jnp.ones((ONES_ROWS, att_tile), F32)
        for j in range(tm // att_tile):
            vt = v[j * att_tile:(j + 1) * att_tile, :].T
            vt_ref[j] = jnp.concatenate(
                [part for h in range(N_HEADS) for part in (vt[h * D_V:(h + 1) * D_V], ones)],
                axis=0).astype(BF16)

    u = _gelu_tanh(u_lin)

    k = seg(D_ATT, 2 * D_ATT)
    _store_heads(kf_ref, k, N_HEADS)
    kb_ref[...] = k.astype(BF16)
    q_ref[...] = (seg(0, D_ATT) * (QK_SCALE * LOG2E)).astype(BF16)

    ri = lax.broadcasted_iota(jnp.int32, (chunk, chunk), 0)
    ci = lax.broadcasted_iota(jnp.int32, (chunk, chunk), 1)
    keep = (ri // CHUNK) >= (ci // CHUNK)
    ws = [jnp.where(keep, ws_ref[i, :chunk, :chunk], 0.0).astype(BF16) for i in range(N_GROUPS)]
    bsp = bsp_ref[...]
    for c in range(tm // chunk):
        r0 = c * chunk
        sp = jnp.concatenate(
            [jnp.dot(ws[i], gnb[r0:r0 + chunk, i * GROUP_DIM:(i + 1) * GROUP_DIM],
                     preferred_element_type=F32) for i in range(N_GROUPS)], axis=1)
        gm_ref[r0:r0 + chunk, :] = (u[r0:r0 + chunk, :] * (sp + bsp)).astype(BF16)


def _diff_lambda(lamv_ref, lam_init):
    lv = lamv_ref[...]
    s1 = jnp.sum(lv[0:1, :] * lv[1:2, :], axis=-1, keepdims=True)
    s2 = jnp.sum(lv[2:3, :] * lv[3:4, :], axis=-1, keepdims=True)
    return jnp.exp(s1) - jnp.exp(s2) + lam_init


def _stack_halves(qh):
    lane = lax.broadcasted_iota(jnp.int32, qh.shape, 1)
    zero = jnp.zeros_like(qh)
    return jnp.concatenate([jnp.where(lane < D_QK, qh, zero), jnp.where(lane >= D_QK, qh, zero)], axis=0)


def _attn_prompt_kernel(lam_ref, gaint_ref, q_ref, k_ref, vt_ref, bias_ref, o_ref,
                        qz_ref, acc_ref, m_even, m_odd, s_even, s_odd, *, t, n_bias, lam_init):
    qi = pl.program_id(0)
    bb = pl.program_id(1)
    nb = pl.num_programs(1) - 1
    n_tiles = qi + 1
    has1 = bb < nb
    has2 = bb >= 1

    def pass1_head(s_ref, m_ref, h, kj, width):
        off = pl.multiple_of(kj * t, t)
        s = lax.dot_general(k_ref[pl.ds(off, width * t), h * D_V:(h + 1) * D_V], qz_ref[h], _NT,
                            preferred_element_type=F32)
        tiles = [bias_ref[h, jnp.minimum(qi - (kj + i), n_bias - 1)] for i in range(width)]
        s = s + jnp.concatenate([jnp.concatenate([b, b], axis=1) for b in tiles], axis=0)
        s_ref[h, pl.ds(off, width * t), :] = s
        m_ref[h] = jnp.maximum(m_ref[h], jnp.max(s, axis=0, keepdims=True))

    def pass2_head(s_ref, m_ref, h, kj, width):
        off = pl.multiple_of(kj * t, t)
        p = jnp.exp2(s_ref[h, pl.ds(off, width * t), :] - m_ref[h])
        vt = jnp.concatenate([vt_ref[kj + i, h * VT_ROWS:(h + 1) * VT_ROWS, :] for i in range(width)],
                             axis=1)
        acc_ref[h] = acc_ref[h] + jnp.dot(vt, p.astype(BF16), preferred_element_type=F32)

    def over_key_tiles(per_head_steps):
        def tiles(kj, width):
            for h in range(N_HEADS):
                for step in per_head_steps:
                    step(h, kj, width)

        widest = KEY_TILES_PER_TRIP

        def trip(i, carry):
            tiles(widest * i, widest)
            return carry

        lax.fori_loop(0, n_tiles // widest, trip, 0)
        done = (n_tiles // widest) * widest
        width = widest // 2
        while width >= 1:
            @pl.when((n_tiles & width) != 0)
            def _(width=width, done=done):
                tiles(done, width)

            done = done + (n_tiles & width)
            width //= 2

    def step(s_new, m_new, s_old, m_old):
        p1 = functools.partial(pass1_head, s_new, m_new)
        p2 = functools.partial(pass2_head, s_old, m_old)

        @pl.when(has1)
        def _():
            for h in range(N_HEADS):
                qz_ref[h] = _stack_halves(q_ref[:, h * D_V:(h + 1) * D_V])
            m_new[...] = jnp.full(m_new.shape, -jnp.inf, F32)

        @pl.when(jnp.logical_and(has1, has2))
        def _():
            over_key_tiles((p1, p2))

        @pl.when(jnp.logical_not(has2))
        def _():
            over_key_tiles((p1,))

        @pl.when(jnp.logical_not(has1))
        def _():
            over_key_tiles((p2,))

    @pl.when(has2)
    def _():
        acc_ref[...] = jnp.zeros(acc_ref.shape, F32)

    @pl.when(jnp.logical_not(has2))
    def _():
        o_ref[...] = jnp.zeros(o_ref.shape, o_ref.dtype)

    @pl.when(bb % 2 == 0)
    def _():
        step(s_even, m_even, s_odd, m_odd)

    @pl.when(bb % 2 == 1)
    def _():
        step(s_odd, m_odd, s_even, m_even)

    @pl.when(has2)
    def _():
        lam = lam_ref[0, 0]
        for h in range(N_HEADS):
            acc = acc_ref[h]
            o = acc[:D_V] / acc[D_V:D_V + 1]
            od = o[:, :t] - lam * o[:, t:]
            ms = jnp.mean(od * od, axis=0, keepdims=True)
            y = od * lax.rsqrt(ms + LN_EPS) * gaint_ref[:, h:h + 1] * (1.0 - lam_init)
            o_ref[:, h * D_V:(h + 1) * D_V] = y.T.astype(BF16)


def _sample_attention_chunk(chunk, n_chunks, lam_ref, gain_ref, q_ref, kn_ref, vn_ref, kc_ref, vc_ref,
                            bias_ref, biasn_ref, o_ref, qz_ref, m_ref, acc_ref, s_ref, *, n_varying, lam_init):
    ts = q_ref.shape[0]
    tc = kc_ref.shape[0] // N_HEADS

    def raise_max(h, s):
        m_old = m_ref[h]
        m_new = jnp.maximum(m_old, jnp.max(s, axis=-1, keepdims=True))
        m_ref[h] = m_new
        acc_ref[h] = acc_ref[h] * jnp.exp2(m_old - m_new)
        return m_new

    def weighted_values(p, v):
        v1 = jnp.concatenate([v, jnp.ones(v.shape, BF16)], axis=1)
        return jnp.dot(p.astype(BF16), v1, preferred_element_type=F32)

    def begin():
        @pl.when(chunk == 0)
        def _():
            for h in range(N_HEADS):
                qz_ref[h] = _stack_halves(q_ref[:, h * D_V:(h + 1) * D_V])
            m_ref[...] = jnp.full(m_ref.shape, -jnp.inf, F32)
            acc_ref[...] = jnp.zeros(acc_ref.shape, F32)

    def scores(h):
        bi = jnp.maximum(chunk - (n_chunks - n_varying - 1), 0)
        kh = kc_ref[pl.ds(h, tc, stride=N_HEADS), :].astype(BF16)
        s = lax.dot_general(qz_ref[h], kh, _NT, preferred_element_type=F32)
        bias = bias_ref[h, bi]
        s = s + jnp.concatenate([bias, bias], axis=0)
        s_ref[h] = s
        raise_max(h, s)

    def values(h):
        p = jnp.exp2(s_ref[h] - m_ref[h])
        vh = vc_ref[pl.ds(h, tc, stride=N_HEADS), :].astype(BF16)
        acc_ref[h] = acc_ref[h] + weighted_values(p, vh)

    def finish():
        @pl.when(chunk == n_chunks - 1)
        def _():
            lam = lam_ref[0, 0]
            heads = [slice(h * D_V, (h + 1) * D_V) for h in range(N_HEADS)]
            scores_new = []
            for h, hs in enumerate(heads):
                s = lax.dot_general(qz_ref[h], kn_ref[:, hs], _NT, preferred_element_type=F32)
                bias = biasn_ref[h]
                scores_new.append(s + jnp.concatenate([bias, bias], axis=0))
            accs = []
            for h, hs in enumerate(heads):
                m_new = raise_max(h, scores_new[h])
                accs.append(acc_ref[h] + weighted_values(jnp.exp2(scores_new[h] - m_new), vn_ref[:, hs]))
            for h, hs in enumerate(heads):
                acc = accs[h]
                o = acc[:, :D_V] / acc[:, D_V:D_V + 1]
                od = o[:ts] - lam * o[ts:]
                ms = jnp.mean(od * od, axis=-1, keepdims=True)
                y = od * lax.rsqrt(ms + LN_EPS) * gain_ref[h:h + 1, :] * (1.0 - lam_init)
                o_ref[:, hs] = y.astype(BF16)

    per_head = lambda fn: [functools.partial(fn, h) for h in range(N_HEADS)]
    return begin, per_head(scores), per_head(values), finish


_N_SAMPLE_INPUTS = 9


def _ffn_up(x1b, wu_ref, c):
    hc = jnp.dot(x1b, wu_ref[:, c * FFN_CHUNK:(c + 1) * FFN_CHUNK], preferred_element_type=F32)
    return jnp.square(jnp.maximum(hc, 0.0)).astype(BF16)


def _ffn_down(hc, wd_ref, c):
    return jnp.dot(hc, wd_ref[c * FFN_CHUNK:(c + 1) * FFN_CHUNK, :], preferred_element_type=F32)


def _finish_kernel(x_ref, a_ref, gm_ref, wo_ref, g1_ref, b1_ref, wu_ref, wd_ref, g2_ref, b2_ref, *rest,
                   alpha, sample_cfg):
    between = {}
    sample_finish = None
    if sample_cfg is None:
        (y_ref,) = rest
    else:
        y_ref = rest[_N_SAMPLE_INPUTS]
        per_batch = sample_cfg["chunks_per_batch"]
        sample_begin, between[0], between[1], sample_finish = _sample_attention_chunk(
            pl.program_id(0) % per_batch, per_batch, *rest[:_N_SAMPLE_INPUTS], *rest[_N_SAMPLE_INPUTS + 1:],
            n_varying=sample_cfg["n_varying"], lam_init=sample_cfg["lam_init"])
        sample_begin()

    tm = x_ref.shape[0]
    bounds = [0] + [tm * f // 4 for f in FINISH_PART_QUARTERS] + [tm]
    parts = [slice(lo, hi) for lo, hi in zip(bounds[:-1], bounds[1:])]

    def out_proj_ln(rs):
        mixed = jnp.concatenate([a_ref[rs, :], gm_ref[rs, :]], axis=1)
        r = jnp.dot(mixed, wo_ref[...], preferred_element_type=F32)
        return _layer_norm(alpha * x_ref[rs, :] + r, g1_ref[...], b1_ref[...])

    def ffn(x1, between=()):
        x1b = x1.astype(BF16)
        f = alpha * x1
        for c in range(wu_ref.shape[1] // FFN_CHUNK):
            f = f + _ffn_down(_ffn_up(x1b, wu_ref, c), wd_ref, c)
            if c < len(between):
                between[c]()
        return f

    x1 = out_proj_ln(parts[0])
    for j, rs in enumerate(parts):
        x1_next = out_proj_ln(parts[j + 1]) if j + 1 < len(parts) else None
        y_ref[rs, :] = _layer_norm(ffn(x1, between.get(j, ())), g2_ref[...], b2_ref[...])
        x1 = x1_next
    if sample_finish is not None:
        sample_finish()


def _const_spec(shape):
    nd = len(shape)
    return pl.BlockSpec(shape, lambda *_: (0,) * nd, pipeline_mode=pl.Buffered(1))


def _params(*sem):
    return pltpu.CompilerParams(dimension_semantics=sem, vmem_limit_bytes=VMEM_LIMIT)


def _project(x2d, w_in, lng, lnb, w_spatial, bsp, *, chunk, att_tile, sample, cast_along=()):
    n, d = x2d.shape
    tm = SAMPLE_ROW_TILE if sample else ROW_TILE
    steps = n // tm
    row = lambda width: pl.BlockSpec((tm, width), lambda i: (i, 0))
    per_head = pl.BlockSpec((tm * N_HEADS, D_V), lambda i: (i, 0))
    out_shape = [jax.ShapeDtypeStruct((n, D_ATT), BF16), jax.ShapeDtypeStruct((n * N_HEADS, D_V), F32),
                 jax.ShapeDtypeStruct((n * N_HEADS, D_V), F32), jax.ShapeDtypeStruct((n, D_ATT), BF16)]
    out_specs = [row(D_ATT), per_head, per_head, row(D_ATT)]
    if sample:
        out_shape += [jax.ShapeDtypeStruct((n, D_ATT), BF16), jax.ShapeDtypeStruct((n, D_GMLP), BF16),
                      jax.ShapeDtypeStruct((n * N_GROUPS, GROUP_DIM), F32)]
        out_specs += [row(D_ATT), row(D_GMLP), pl.BlockSpec((tm * N_GROUPS, GROUP_DIM), lambda i: (i, 0))]
    else:
        out_shape += [jax.ShapeDtypeStruct((n // att_tile, N_HEADS * VT_ROWS, att_tile), BF16),
                      jax.ShapeDtypeStruct((n, D_GMLP), BF16)]
        out_specs += [pl.BlockSpec((tm // att_tile, N_HEADS * VT_ROWS, att_tile), lambda i: (i, 0, 0)), row(D_GMLP)]
    cast_specs = [pl.BlockSpec((w.shape[0] // steps, w.shape[1]), lambda i: (i, 0)) for w in cast_along]
    out_shape += [jax.ShapeDtypeStruct(w.shape, BF16) for w in cast_along]
    return pl.pallas_call(
        functools.partial(_proj_kernel, chunk=chunk, att_tile=att_tile, sample=sample, n_cast=len(cast_along)),
        grid=(steps,),
        in_specs=[row(d), _const_spec(w_in.shape), _const_spec(lng.shape), _const_spec(lnb.shape),
                  _const_spec(w_spatial.shape), _const_spec(bsp.shape)] + cast_specs,
        out_specs=out_specs + cast_specs,
        out_shape=out_shape,
        compiler_params=_params("parallel"),
        name="proj_sample" if sample else "proj_prompt",
    )(x2d, w_in, lng, lnb, w_spatial, bsp, *cast_along)


def _finish(x2d, attn, gm, w_out, g1, b1, w_up, w_down, g2, b2, *, alpha, name, sample=None):
    n, d = x2d.shape
    tm = min(FINISH_ROW_TILE, n)
    nt = n // tm
    row = lambda width: pl.BlockSpec((tm, width), lambda i: (i, 0))
    in_specs = [row(d), row(D_ATT), row(D_GMLP), _const_spec(w_out.shape), _const_spec(g1.shape),
                _const_spec(b1.shape), _const_spec(w_up.shape), _const_spec(w_down.shape),
                _const_spec(g2.shape), _const_spec(b2.shape)]
    operands = [x2d, attn, gm, w_out, g1, b1, w_up, w_down, g2, b2]
    out_specs = [row(d)]
    out_shape = [jax.ShapeDtypeStruct((n, d), F32)]
    scratch = []
    sample_cfg = None
    if sample is not None:
        ck, cv = sample["cache_k"], sample["cache_v"]
        b_s, tc = ck.shape[0], sample["cache_tile"]
        per_batch = ck.shape[1] // (tc * N_HEADS)
        assert nt == b_s * per_batch, "one cache chunk per grid step"
        t_s = sample["q"].shape[0] // b_s
        new_spec = pl.BlockSpec((t_s, D_ATT), lambda i: (i // per_batch, 0))
        cache_spec = pl.BlockSpec((None, tc * N_HEADS, D_V), lambda i: (i // per_batch, i % per_batch, 0))
        tiles = [sample["bias_cache"], sample["bias_new"]]
        in_specs += ([pl.BlockSpec(memory_space=pltpu.SMEM), _const_spec(sample["gain"].shape)] + [new_spec] * 3
                     + [cache_spec] * 2 + [_const_spec(a.shape) for a in tiles])
        operands += [sample["lam"], sample["gain"], sample["q"], sample["k_new"], sample["v_new"], ck, cv] + tiles
        out_specs.append(new_spec)
        out_shape.append(jax.ShapeDtypeStruct((b_s * t_s, D_ATT), BF16))
        scratch = [pltpu.VMEM((N_HEADS, 2 * t_s, D_V), BF16), pltpu.VMEM((N_HEADS, 2 * t_s, 1), F32),
                   pltpu.VMEM((N_HEADS, 2 * t_s, 2 * D_V), F32), pltpu.VMEM((N_HEADS, 2 * t_s, tc), F32)]
        sample_cfg = dict(chunks_per_batch=per_batch, n_varying=sample["n_varying"],
                          lam_init=sample["lam_init"])
    outs = pl.pallas_call(
        functools.partial(_finish_kernel, alpha=alpha, sample_cfg=sample_cfg),
        grid=(nt,),
        in_specs=in_specs,
        out_specs=out_specs,
        out_shape=out_shape,
        scratch_shapes=scratch,
        compiler_params=pltpu.CompilerParams(
            dimension_semantics=("arbitrary",),
            vmem_limit_bytes=FUSED_VMEM_LIMIT if sample is not None else VMEM_LIMIT),
        name=name,
    )(*operands)
    return outs if sample is not None else outs[0]


def kernel(x_prompt, x_sample, cache_k, cache_v, rel_bias_table, w_in, lambda_q1, lambda_k1, lambda_q2,
           lambda_k2, subln_gain, gmlp_ln_gain, gmlp_ln_bias, w_spatial, b_spatial, w_out, ln1_gain,
           ln1_bias, w_ffn_up, w_ffn_down, ln2_gain, ln2_bias):
    depth = w_in.shape[0]
    assert depth == 1, "single-layer step"
    b_p, t_p, d_model = x_prompt.shape
    b_s, t_s, _ = x_sample.shape
    past = cache_k.shape[2]
    alpha = (2.0 * depth) ** 0.25
    lam_init = _lambda_init(0)
    t = ATT_TILE
    tc = CACHE_TILE
    assert t_p % t == 0 and past % tc == 0 and t % CHUNK == 0 and t_s == CHUNK and past % GMLP_CHUNK == 0
    assert ROW_TILE % t == 0 and (b_p * t_p) % ROW_TILE == 0 and (b_s * t_s) % SAMPLE_ROW_TILE == 0
    assert SAMPLE_ROW_TILE % t_s == 0

    bucket_table = _bucket_of_distance(past + t_s)
    n_prompt_bias = _num_varying_tiles(t, bucket_table) + 1
    n_sample_var = _num_varying_tiles(tc, bucket_table) - 1
    n_sample_bias = n_sample_var + 1
    assert past // tc > n_sample_var

    lamv = jnp.stack([lambda_q1[0], lambda_k1[0], lambda_q2[0], lambda_k2[0]], axis=0)
    bias_p, bias_s, bias_n, lam = pl.pallas_call(
        functools.partial(_bias_kernel, n_prompt=n_prompt_bias, t=t, n_sample=n_sample_bias, tc=tc,
                          past=past, ts=t_s, bucket_table=bucket_table, lam_init=lam_init),
        in_specs=[pl.BlockSpec(memory_space=pltpu.SMEM), pl.BlockSpec(memory_space=pltpu.VMEM)],
        out_shape=[jax.ShapeDtypeStruct((N_HEADS, n_prompt_bias, t, t), F32),
                   jax.ShapeDtypeStruct((N_HEADS, n_sample_bias, t_s, tc), F32),
                   jax.ShapeDtypeStruct((N_HEADS, t_s, t_s), F32),
                   jax.ShapeDtypeStruct((8, 128), F32)],
        compiler_params=pltpu.CompilerParams(vmem_limit_bytes=VMEM_LIMIT),
        name="rel_bias_tiles",
    )(rel_bias_table, lamv)

    w_in_b = w_in[0].astype(BF16)
    lng = gmlp_ln_gain[0].reshape(1, D_GMLP)
    lnb = gmlp_ln_bias[0].reshape(1, D_GMLP)
    bsp = jnp.repeat(b_spatial[0].T, GROUP_DIM, axis=1)
    g1, b1 = ln1_gain[0].reshape(1, d_model), ln1_bias[0].reshape(1, d_model)
    g2, b2 = ln2_gain[0].reshape(1, d_model), ln2_bias[0].reshape(1, d_model)
    gain = subln_gain[0]

    xp = x_prompt.reshape(b_p * t_p, d_model)
    (q_p, k_p, v_p, kb_p, vt_p, gm_p, w_out_b, w_up_b, w_down_b) = _project(
        xp, w_in_b, lng, lnb, w_spatial[0], bsp, chunk=GMLP_CHUNK, att_tile=t, sample=False,
        cast_along=(w_out[0], w_ffn_up[0], w_ffn_down[0]))
    nq = t_p // t
    gain_t = gain.T
    attn_p = pl.pallas_call(
        functools.partial(_attn_prompt_kernel, t=t, n_bias=n_prompt_bias, lam_init=lam_init),
        grid=(nq, b_p + 1),
        in_specs=[pl.BlockSpec(memory_space=pltpu.SMEM), _const_spec(gain_t.shape),
                  pl.BlockSpec((t, D_ATT), lambda i, bb: (jnp.minimum(bb, b_p - 1) * nq + i, 0)),
                  pl.BlockSpec((t_p, D_ATT), lambda i, bb: (jnp.minimum(bb, b_p - 1), 0)),
                  pl.BlockSpec((nq, N_HEADS * VT_ROWS, t), lambda i, bb: (jnp.maximum(bb - 1, 0), 0, 0)),
                  _const_spec(bias_p.shape)],
        out_specs=pl.BlockSpec((t, D_ATT), lambda i, bb: (jnp.maximum(bb - 1, 0) * nq + i, 0)),
        out_shape=jax.ShapeDtypeStruct((b_p * t_p, D_ATT), BF16),
        scratch_shapes=[pltpu.VMEM((N_HEADS, 2 * t, D_V), BF16), pltpu.VMEM((N_HEADS, VT_ROWS, 2 * t), F32),
                        pltpu.VMEM((N_HEADS, 1, 2 * t), F32), pltpu.VMEM((N_HEADS, 1, 2 * t), F32),
                        pltpu.VMEM((N_HEADS, t_p, 2 * t), F32), pltpu.VMEM((N_HEADS, t_p, 2 * t), F32)],
        compiler_params=_params("arbitrary", "arbitrary"),
        name="attn_prompt",
    )(lam, gain_t, q_p, kb_p, vt_p, bias_p)

    xs = x_sample.reshape(b_s * t_s, d_model)
    q_s, k_s, v_s, kb_s, vb_s, gm_s, gn_s = _project(xs, w_in_b, lng, lnb, w_spatial[0], bsp[:t_s],
                                                     chunk=t_s, att_tile=t, sample=True)
    sample = dict(lam=lam, gain=gain, q=q_s, k_new=kb_s, v_new=vb_s,
                  cache_k=cache_k.reshape(b_s, past * N_HEADS, D_V),
                  cache_v=cache_v.reshape(b_s, past * N_HEADS, D_V), cache_tile=tc,
                  bias_cache=bias_s, bias_new=bias_n, n_varying=n_sample_var, lam_init=lam_init)
    y_p, attn_s = _finish(xp, attn_p, gm_p, w_out_b, g1, b1, w_up_b, w_down_b, g2, b2, alpha=alpha,
                          name="finish_prompt_attn_sample", sample=sample)
    y_s = _finish(xs, attn_s, gm_s, w_out_b, g1, b1, w_up_b, w_down_b, g2, b2, alpha=alpha,
                  name="finish_sample")

    head_shape_p = (depth, b_p, t_p, N_HEADS, D_V)
    head_shape_s = (depth, b_s, t_s, N_HEADS, D_V)
    return (y_p.reshape(b_p, t_p, d_model), y_s.reshape(b_s, t_s, d_model),
            k_p.reshape(head_shape_p), v_p.reshape(head_shape_p),
            k_s.reshape(head_shape_s), v_s.reshape(head_shape_s),
            gn_s.reshape(depth, b_s, t_s, N_GROUPS, GROUP_DIM))
```

```python
import functools
import math

import numpy as np
import jax
import jax.numpy as jnp
from jax import lax
from jax.experimental import pallas as pl
from jax.experimental.pallas import tpu as pltpu

F32 = jnp.float32
BF16 = jnp.bfloat16

CHUNK = 64
N_HEADS = 4
D_QK = 64
D_V = 2 * D_QK
D_ATT = N_HEADS * D_V
N_GROUPS = 4
GROUP_DIM = 128
D_GMLP = N_GROUPS * GROUP_DIM
GMLP_CHUNK = 128
NUM_BUCKETS = 32
MAX_DISTANCE = 1024
LN_EPS = 1e-5
NEG_INF = -1e30
QK_SCALE = D_QK ** -0.5
LOG2E = math.log2(math.e)
ONES_ROWS = 16
VT_ROWS = D_V + ONES_ROWS

ROW_TILE = 1024
SAMPLE_ROW_TILE = 512
FINISH_ROW_TILE = 512
FINISH_PART_QUARTERS = (2,)
ATT_TILE = 256
KEY_TILES_PER_TRIP = 2
CACHE_TILE = 2048
FFN_CHUNK = 1024
VMEM_LIMIT = 56 * 1024 * 1024
FUSED_VMEM_LIMIT = 60 * 1024 * 1024

_NT = (((1,), (1,)), ((), ()))


def _lambda_init(layer):
    return 0.8 - 0.6 * math.exp(-0.3 * layer)


def _bucket_of_distance(max_n):
    half = NUM_BUCKETS // 2
    max_exact = half // 2
    n = np.arange(max_n + 1)

    def table(dtype):
        nf = np.maximum(n, 1).astype(dtype)
        scaled = np.log(nf / dtype(max_exact)) / dtype(math.log(MAX_DISTANCE / max_exact)) * dtype(half - max_exact)
        large = np.minimum(max_exact + scaled.astype(np.int64), half - 1)
        return np.where(n < max_exact, n, large)

    t64 = table(np.float64)
    assert np.array_equal(t64, table(np.float32)), "bucket boundaries are precision sensitive"
    return t64


def _rel_bucket_static(rel, bucket_table):
    return (NUM_BUCKETS // 2 if rel > 0 else 0) + int(bucket_table[abs(rel)])


def _bias_tile(tab_ref, head, d0, nrows, ncols, bucket_table, *, keys_on_rows=False):
    row = lax.broadcasted_iota(jnp.int32, (nrows, ncols), 0)
    col = lax.broadcasted_iota(jnp.int32, (nrows, ncols), 1)
    if keys_on_rows:
        d = (row - col) + d0
        dmin, dmax = d0 - (ncols - 1), d0 + nrows - 1
    else:
        d = (col - row) + d0
        dmin, dmax = d0 - (nrows - 1), d0 + ncols - 1
    prev = _rel_bucket_static(dmin, bucket_table)
    val = jnp.full((nrows, ncols), tab_ref[prev, head] * LOG2E, F32)
    for dd in range(dmin + 1, dmax + 1):
        b = _rel_bucket_static(dd, bucket_table)
        if b != prev:
            val = jnp.where(d >= dd, tab_ref[b, head] * LOG2E, val)
            prev = b
    return val


def _bias_kernel(tab_ref, lamv_ref, bp_ref, bs_ref, bn_ref, lam_ref, *, n_prompt, t, n_sample, tc, past, ts,
                 bucket_table, lam_init):
    lam_ref[...] = jnp.broadcast_to(_diff_lambda(lamv_ref, lam_init), lam_ref.shape)
    key = lax.broadcasted_iota(jnp.int32, (t, t), 0)
    qry = lax.broadcasted_iota(jnp.int32, (t, t), 1)
    visible = (key // CHUNK) <= (qry // CHUNK)
    n_cache_tiles = past // tc
    for h in range(N_HEADS):
        for dl in range(n_prompt):
            tile = _bias_tile(tab_ref, h, -dl * t, t, t, bucket_table, keys_on_rows=True)
            if dl == 0:
                tile = jnp.where(visible, tile, NEG_INF)
            bp_ref[h, dl] = tile
        for i in range(n_sample):
            j = n_cache_tiles - n_sample + i
            bs_ref[h, i] = _bias_tile(tab_ref, h, j * tc - past, ts, tc, bucket_table)
        bn_ref[h] = _bias_tile(tab_ref, h, 0, ts, ts, bucket_table)


def _far_distance(bucket_table):
    n = len(bucket_table) - 1
    while n > 0 and bucket_table[n - 1] == bucket_table[-1]:
        n -= 1
    return n


def _num_varying_tiles(tile, bucket_table):
    far = _far_distance(bucket_table)
    n = 1
    while (n - 1) * tile + 1 < far:
        n += 1
    return n


def _gelu_tanh(x):
    c = math.sqrt(2.0 / math.pi)
    return 0.5 * x * (1.0 + jnp.tanh(c * (x + 0.044715 * (x * x * x))))


def _layer_norm(x, g, b):
    mu = jnp.mean(x, axis=-1, keepdims=True)
    xc = x - mu
    var = jnp.mean(xc * xc, axis=-1, keepdims=True)
    return xc * lax.rsqrt(var + LN_EPS) * g + b


def _store_heads(ref, x, n):
    rows = x.shape[0]
    for h in range(n):
        ref[pl.ds(h, rows, stride=n), :] = x[:, h * 128:(h + 1) * 128]


def _proj_kernel(x_ref, w_ref, lng_ref, lnb_ref, ws_ref, bsp_ref, *refs, chunk, att_tile, sample, n_cast):
    for src, dst in zip(refs[:n_cast], refs[len(refs) - n_cast:]):
        dst[...] = src[...].astype(BF16)
    out_refs = refs[n_cast:len(refs) - n_cast]
    if sample:
        q_ref, kf_ref, vf_ref, kb_ref, vb_ref, gm_ref, gn_ref = out_refs
    else:
        q_ref, kf_ref, vf_ref, kb_ref, vt_ref, gm_ref = out_refs
    tm = x_ref.shape[0]
    xb = x_ref[...].astype(BF16)

    def seg(lo, hi):
        return jnp.dot(xb, w_ref[:, lo:hi], preferred_element_type=F32)

    g_lin = seg(3 * D_ATT + D_GMLP, 3 * D_ATT + 2 * D_GMLP)
    u_lin = seg(3 * D_ATT, 3 * D_ATT + D_GMLP)

    lng = lng_ref[...]
    lnb = lnb_ref[...]
    g = _gelu_tanh(g_lin)
    gn = jnp.concatenate(
        [_layer_norm(g[:, i * GROUP_DIM:(i + 1) * GROUP_DIM],
                     lng[:, i * GROUP_DIM:(i + 1) * GROUP_DIM],
                     lnb[:, i * GROUP_DIM:(i + 1) * GROUP_DIM]) for i in range(N_GROUPS)], axis=1)
    if sample:
        _store_heads(gn_ref, gn, N_GROUPS)
    gnb = gn.astype(BF16)

    v = seg(2 * D_ATT, 3 * D_ATT)
    _store_heads(vf_ref, v, N_HEADS)
    if sample:
        vb_ref[...] = v.astype(BF16)
    else:
        ones = jnp.ones((ONES_ROWS, att_tile), F32)
        for j in range(tm // att_tile):
            vt = v[j * att_tile:(j + 1) * att_tile, :].T
            vt_ref[j] = jnp.concatenate(
                [part for h in range(N_HEADS) for part in (vt[h * D_V:(h + 1) * D_V], ones)],
                axis=0).astype(BF16)

    u = _gelu_tanh(u_lin)

    k = seg(D_ATT, 2 * D_ATT)
    _store_heads(kf_ref, k, N_HEADS)
    kb_ref[...] = k.astype(BF16)
    q_ref[...] = (seg(0, D_ATT) * (QK_SCALE * LOG2E)).astype(BF16)

    ri = lax.broadcasted_iota(jnp.int32, (chunk, chunk), 0)
    ci = lax.broadcasted_iota(jnp.int32, (chunk, chunk), 1)
    keep = (ri // CHUNK) >= (ci // CHUNK)
    ws = [jnp.where(keep, ws_ref[i, :chunk, :chunk], 0.0).astype(BF16) for i in range(N_GROUPS)]
    bsp = bsp_ref[...]
    for c in range(tm // chunk):
        r0 = c * chunk
        sp = jnp.concatenate(
            [jnp.dot(ws[i], gnb[r0:r0 + chunk, i * GROUP_DIM:(i + 1) * GROUP_DIM],
                     preferred_element_type=F32) for i in range(N_GROUPS)], axis=1)
        gm_ref[r0:r0 + chunk, :] = (u[r0:r0 + chunk, :] * (sp + bsp)).astype(BF16)


def _diff_lambda(lamv_ref, lam_init):
    lv = lamv_ref[...]
    s1 = jnp.sum(lv[0:1, :] * lv[1:2, :], axis=-1, keepdims=True)
    s2 = jnp.sum(lv[2:3, :] * lv[3:4, :], axis=-1, keepdims=True)
    return jnp.exp(s1) - jnp.exp(s2) + lam_init


def _stack_halves(qh):
    lane = lax.broadcasted_iota(jnp.int32, qh.shape, 1)
    zero = jnp.zeros_like(qh)
    return jnp.concatenate([jnp.where(lane < D_QK, qh, zero), jnp.where(lane >= D_QK, qh, zero)], axis=0)


def _attn_prompt_kernel(lam_ref, gaint_ref, q_ref, k_ref, vt_ref, bias_ref, o_ref,
                        qz_ref, acc_ref, m_even, m_odd, s_even, s_odd, *, t, n_bias, lam_init):
    qi = pl.program_id(0)
    bb = pl.program_id(1)
    nb = pl.num_programs(1) - 1
    n_tiles = qi + 1
    has1 = bb < nb
    has2 = bb >= 1

    def pass1_head(s_ref, m_ref, h, kj, width):
        off = pl.multiple_of(kj * t, t)
        s = lax.dot_general(k_ref[pl.ds(off, width * t), h * D_V:(h + 1) * D_V], qz_ref[h], _NT,
                            preferred_element_type=F32)
        tiles = [bias_ref[h, jnp.minimum(qi - (kj + i), n_bias - 1)] for i in range(width)]
        s = s + jnp.concatenate([jnp.concatenate([b, b], axis=1) for b in tiles], axis=0)
        s_ref[h, pl.ds(off, width * t), :] = s
        m_ref[h] = jnp.maximum(m_ref[h], jnp.max(s, axis=0, keepdims=True))

    def pass2_head(s_ref, m_ref, h, kj, width):
        off = pl.multiple_of(kj * t, t)
        p = jnp.exp2(s_ref[h, pl.ds(off, width * t), :] - m_ref[h])
        vt = jnp.concatenate([vt_ref[kj + i, h * VT_ROWS:(h + 1) * VT_ROWS, :] for i in range(width)],
                             axis=1)
        acc_ref[h] = acc_ref[h] + jnp.dot(vt, p.astype(BF16), preferred_element_type=F32)

    def over_key_tiles(per_head_steps):
        def tiles(kj, width):
            for h in range(N_HEADS):
                for step in per_head_steps:
                    step(h, kj, width)

        widest = KEY_TILES_PER_TRIP

        def trip(i, carry):
            tiles(widest * i, widest)
            return carry

        lax.fori_loop(0, n_tiles // widest, trip, 0)
        done = (n_tiles // widest) * widest
        width = widest // 2
        while width >= 1:
            @pl.when((n_tiles & width) != 0)
            def _(width=width, done=done):
                tiles(done, width)

            done = done + (n_tiles & width)
            width //= 2

    def step(s_new, m_new, s_old, m_old):
        p1 = functools.partial(pass1_head, s_new, m_new)
        p2 = functools.partial(pass2_head, s_old, m_old)

        @pl.when(has1)
        def _():
            for h in range(N_HEADS):
                qz_ref[h] = _stack_halves(q_ref[:, h * D_V:(h + 1) * D_V])
            m_new[...] = jnp.full(m_new.shape, -jnp.inf, F32)

        @pl.when(jnp.logical_and(has1, has2))
        def _():
            over_key_tiles((p1, p2))

        @pl.when(jnp.logical_not(has2))
        def _():
            over_key_tiles((p1,))

        @pl.when(jnp.logical_not(has1))
        def _():
            over_key_tiles((p2,))

    @pl.when(has2)
    def _():
        acc_ref[...] = jnp.zeros(acc_ref.shape, F32)

    @pl.when(jnp.logical_not(has2))
    def _():
        o_ref[...] = jnp.zeros(o_ref.shape, o_ref.dtype)

    @pl.when(bb % 2 == 0)
    def _():
        step(s_even, m_even, s_odd, m_odd)

    @pl.when(bb % 2 == 1)
    def _():
        step(s_odd, m_odd, s_even, m_even)

    @pl.when(has2)
    def _():
        lam = lam_ref[0, 0]
        eye = (lax.broadcasted_iota(jnp.int32, (t, t), 0)
               == lax.broadcasted_iota(jnp.int32, (t, t), 1)).astype(BF16)
        for h in range(N_HEADS):
            acc = acc_ref[h]
            inv = 1.0 / acc[D_V:D_V + 1]
            od = acc[:D_V, :t] * inv[:, :t] - acc[:D_V, t:] * (lam * inv[:, t:])
            ms = jnp.mean(od * od, axis=0, keepdims=True)
            y = (od * (lax.rsqrt(ms + LN_EPS) * (1.0 - lam_init))) * gaint_ref[h]
            o_ref[:, h * D_V:(h + 1) * D_V] = lax.dot_general(
                eye, y.astype(BF16), _NT, preferred_element_type=F32).astype(BF16)


def _sample_attention_chunk(chunk, n_chunks, lam_ref, gain_ref, q_ref, kn_ref, vn_ref, kc_ref, vc_ref,
                            bias_ref, biasn_ref, o_ref, qz_ref, m_ref, acc_ref, s_ref, *, n_varying, lam_init):
    ts = q_ref.shape[0]
    tc = kc_ref.shape[0] // N_HEADS

    def raise_max(h, s):
        m_old = m_ref[h]
        m_new = jnp.maximum(m_old, jnp.max(s, axis=-1, keepdims=True))
        m_ref[h] = m_new
        acc_ref[h] = acc_ref[h] * jnp.exp2(m_old - m_new)
        return m_new

    def weighted_values(p, v):
        v1 = jnp.concatenate([v, jnp.ones(v.shape, BF16)], axis=1)
        return jnp.dot(p.astype(BF16), v1, preferred_element_type=F32)

    def begin():
        @pl.when(chunk == 0)
        def _():
            for h in range(N_HEADS):
                qz_ref[h] = _stack_halves(q_ref[:, h * D_V:(h + 1) * D_V])
            m_ref[...] = jnp.full(m_ref.shape, -jnp.inf, F32)
            acc_ref[...] = jnp.zeros(acc_ref.shape, F32)

    def scores(h):
        bi = jnp.maximum(chunk - (n_chunks - n_varying - 1), 0)
        kh = kc_ref[pl.ds(h, tc, stride=N_HEADS), :].astype(BF16)
        s = lax.dot_general(qz_ref[h], kh, _NT, preferred_element_type=F32)
        bias = bias_ref[h, bi]
        s = s + jnp.concatenate([bias, bias], axis=0)
        s_ref[h] = s
        raise_max(h, s)

    def values(h):
        p = jnp.exp2(s_ref[h] - m_ref[h])
        vh = vc_ref[pl.ds(h, tc, stride=N_HEADS), :].astype(BF16)
        acc_ref[h] = acc_ref[h] + weighted_values(p, vh)

    def finish():
        @pl.when(chunk == n_chunks - 1)
        def _():
            lam = lam_ref[0, 0]
            heads = [slice(h * D_V, (h + 1) * D_V) for h in range(N_HEADS)]
            scores_new = []
            for h, hs in enumerate(heads):
                s = lax.dot_general(qz_ref[h], kn_ref[:, hs], _NT, preferred_element_type=F32)
                bias = biasn_ref[h]
                scores_new.append(s + jnp.concatenate([bias, bias], axis=0))
            accs = []
            for h, hs in enumerate(heads):
                m_new = raise_max(h, scores_new[h])
                accs.append(acc_ref[h] + weighted_values(jnp.exp2(scores_new[h] - m_new), vn_ref[:, hs]))
            for h, hs in enumerate(heads):
                acc = accs[h]
                o = acc[:, :D_V] / acc[:, D_V:D_V + 1]
                od = o[:ts] - lam * o[ts:]
                ms = jnp.mean(od * od, axis=-1, keepdims=True)
                y = od * lax.rsqrt(ms + LN_EPS) * gain_ref[h:h + 1, :] * (1.0 - lam_init)
                o_ref[:, hs] = y.astype(BF16)

    per_head = lambda fn: [functools.partial(fn, h) for h in range(N_HEADS)]
    return begin, per_head(scores), per_head(values), finish


_N_SAMPLE_INPUTS = 9


def _ffn_up(x1b, wu_ref, c):
    hc = jnp.dot(x1b, wu_ref[:, c * FFN_CHUNK:(c + 1) * FFN_CHUNK], preferred_element_type=F32)
    return jnp.square(jnp.maximum(hc, 0.0)).astype(BF16)


def _ffn_down(hc, wd_ref, c):
    return jnp.dot(hc, wd_ref[c * FFN_CHUNK:(c + 1) * FFN_CHUNK, :], preferred_element_type=F32)


def _finish_kernel(x_ref, a_ref, gm_ref, wo_ref, g1_ref, b1_ref, wu_ref, wd_ref, g2_ref, b2_ref, *rest,
                   alpha, sample_cfg):
    between = {}
    sample_finish = None
    if sample_cfg is None:
        (y_ref,) = rest
    else:
        y_ref = rest[_N_SAMPLE_INPUTS]
        per_batch = sample_cfg["chunks_per_batch"]
        sample_begin, between[0], between[1], sample_finish = _sample_attention_chunk(
            pl.program_id(0) % per_batch, per_batch, *rest[:_N_SAMPLE_INPUTS], *rest[_N_SAMPLE_INPUTS + 1:],
            n_varying=sample_cfg["n_varying"], lam_init=sample_cfg["lam_init"])
        sample_begin()

    tm = x_ref.shape[0]
    bounds = [0] + [tm * f // 4 for f in FINISH_PART_QUARTERS] + [tm]
    parts = [slice(lo, hi) for lo, hi in zip(bounds[:-1], bounds[1:])]

    def out_proj_ln(rs):
        mixed = jnp.concatenate([a_ref[rs, :], gm_ref[rs, :]], axis=1)
        r = jnp.dot(mixed, wo_ref[...], preferred_element_type=F32)
        return _layer_norm(alpha * x_ref[rs, :] + r, g1_ref[...], b1_ref[...])

    def ffn(x1, between=()):
        x1b = x1.astype(BF16)
        f = alpha * x1
        for c in range(wu_ref.shape[1] // FFN_CHUNK):
            f = f + _ffn_down(_ffn_up(x1b, wu_ref, c), wd_ref, c)
            if c < len(between):
                between[c]()
        return f

    x1 = out_proj_ln(parts[0])
    for j, rs in enumerate(parts):
        x1_next = out_proj_ln(parts[j + 1]) if j + 1 < len(parts) else None
        y_ref[rs, :] = _layer_norm(ffn(x1, between.get(j, ())), g2_ref[...], b2_ref[...])
        x1 = x1_next
    if sample_finish is not None:
        sample_finish()


def _const_spec(shape):
    nd = len(shape)
    return pl.BlockSpec(shape, lambda *_: (0,) * nd, pipeline_mode=pl.Buffered(1))


def _params(*sem):
    return pltpu.CompilerParams(dimension_semantics=sem, vmem_limit_bytes=VMEM_LIMIT)


def _project(x2d, w_in, lng, lnb, w_spatial, bsp, *, chunk, att_tile, sample, cast_along=()):
    n, d = x2d.shape
    tm = SAMPLE_ROW_TILE if sample else ROW_TILE
    steps = n // tm
    row = lambda width: pl.BlockSpec((tm, width), lambda i: (i, 0))
    per_head = pl.BlockSpec((tm * N_HEADS, D_V), lambda i: (i, 0))
    out_shape = [jax.ShapeDtypeStruct((n, D_ATT), BF16), jax.ShapeDtypeStruct((n * N_HEADS, D_V), F32),
                 jax.ShapeDtypeStruct((n * N_HEADS, D_V), F32), jax.ShapeDtypeStruct((n, D_ATT), BF16)]
    out_specs = [row(D_ATT), per_head, per_head, row(D_ATT)]
    if sample:
        out_shape += [jax.ShapeDtypeStruct((n, D_ATT), BF16), jax.ShapeDtypeStruct((n, D_GMLP), BF16),
                      jax.ShapeDtypeStruct((n * N_GROUPS, GROUP_DIM), F32)]
        out_specs += [row(D_ATT), row(D_GMLP), pl.BlockSpec((tm * N_GROUPS, GROUP_DIM), lambda i: (i, 0))]
    else:
        out_shape += [jax.ShapeDtypeStruct((n // att_tile, N_HEADS * VT_ROWS, att_tile), BF16),
                      jax.ShapeDtypeStruct((n, D_GMLP), BF16)]
        out_specs += [pl.BlockSpec((tm // att_tile, N_HEADS * VT_ROWS, att_tile), lambda i: (i, 0, 0)), row(D_GMLP)]
    cast_specs = [pl.BlockSpec((w.shape[0] // steps, w.shape[1]), lambda i: (i, 0)) for w in cast_along]
    out_shape += [jax.ShapeDtypeStruct(w.shape, BF16) for w in cast_along]
    return pl.pallas_call(
        functools.partial(_proj_kernel, chunk=chunk, att_tile=att_tile, sample=sample, n_cast=len(cast_along)),
        grid=(steps,),
        in_specs=[row(d), _const_spec(w_in.shape), _const_spec(lng.shape), _const_spec(lnb.shape),
                  _const_spec(w_spatial.shape), _const_spec(bsp.shape)] + cast_specs,
        out_specs=out_specs + cast_specs,
        out_shape=out_shape,
        compiler_params=_params("parallel"),
        name="proj_sample" if sample else "proj_prompt",
    )(x2d, w_in, lng, lnb, w_spatial, bsp, *cast_along)


def _finish(x2d, attn, gm, w_out, g1, b1, w_up, w_down, g2, b2, *, alpha, name, sample=None):
    n, d = x2d.shape
    tm = min(FINISH_ROW_TILE, n)
    nt = n // tm
    row = lambda width: pl.BlockSpec((tm, width), lambda i: (i, 0))
    in_specs = [row(d), row(D_ATT), row(D_GMLP), _const_spec(w_out.shape), _const_spec(g1.shape),
                _const_spec(b1.shape), _const_spec(w_up.shape), _const_spec(w_down.shape),
                _const_spec(g2.shape), _const_spec(b2.shape)]
    operands = [x2d, attn, gm, w_out, g1, b1, w_up, w_down, g2, b2]
    out_specs = [row(d)]
    out_shape = [jax.ShapeDtypeStruct((n, d), F32)]
    scratch = []
    sample_cfg = None
    if sample is not None:
        ck, cv = sample["cache_k"], sample["cache_v"]
        b_s, tc = ck.shape[0], sample["cache_tile"]
        per_batch = ck.shape[1] // (tc * N_HEADS)
        assert nt == b_s * per_batch, "one cache chunk per grid step"
        t_s = sample["q"].shape[0] // b_s
        new_spec = pl.BlockSpec((t_s, D_ATT), lambda i: (i // per_batch, 0))
        cache_spec = pl.BlockSpec((None, tc * N_HEADS, D_V), lambda i: (i // per_batch, i % per_batch, 0))
        tiles = [sample["bias_cache"], sample["bias_new"]]
        in_specs += ([pl.BlockSpec(memory_space=pltpu.SMEM), _const_spec(sample["gain"].shape)] + [new_spec] * 3
                     + [cache_spec] * 2 + [_const_spec(a.shape) for a in tiles])
        operands += [sample["lam"], sample["gain"], sample["q"], sample["k_new"], sample["v_new"], ck, cv] + tiles
        out_specs.append(new_spec)
        out_shape.append(jax.ShapeDtypeStruct((b_s * t_s, D_ATT), BF16))
        scratch = [pltpu.VMEM((N_HEADS, 2 * t_s, D_V), BF16), pltpu.VMEM((N_HEADS, 2 * t_s, 1), F32),
                   pltpu.VMEM((N_HEADS, 2 * t_s, 2 * D_V), F32), pltpu.VMEM((N_HEADS, 2 * t_s, tc), F32)]
        sample_cfg = dict(chunks_per_batch=per_batch, n_varying=sample["n_varying"],
                          lam_init=sample["lam_init"])
    outs = pl.pallas_call(
        functools.partial(_finish_kernel, alpha=alpha, sample_cfg=sample_cfg),
        grid=(nt,),
        in_specs=in_specs,
        out_specs=out_specs,
        out_shape=out_shape,
        scratch_shapes=scratch,
        compiler_params=pltpu.CompilerParams(
            dimension_semantics=("arbitrary",),
            vmem_limit_bytes=FUSED_VMEM_LIMIT if sample is not None else VMEM_LIMIT),
        name=name,
    )(*operands)
    return outs if sample is not None else outs[0]


def kernel(x_prompt, x_sample, cache_k, cache_v, rel_bias_table, w_in, lambda_q1, lambda_k1, lambda_q2,
           lambda_k2, subln_gain, gmlp_ln_gain, gmlp_ln_bias, w_spatial, b_spatial, w_out, ln1_gain,
           ln1_bias, w_ffn_up, w_ffn_down, ln2_gain, ln2_bias):
    depth = w_in.shape[0]
    assert depth == 1, "single-layer step"
    b_p, t_p, d_model = x_prompt.shape
    b_s, t_s, _ = x_sample.shape
    past = cache_k.shape[2]
    alpha = (2.0 * depth) ** 0.25
    lam_init = _lambda_init(0)
    t = ATT_TILE
    tc = CACHE_TILE
    assert t_p % t == 0 and past % tc == 0 and t % CHUNK == 0 and t_s == CHUNK and past % GMLP_CHUNK == 0
    assert ROW_TILE % t == 0 and (b_p * t_p) % ROW_TILE == 0 and (b_s * t_s) % SAMPLE_ROW_TILE == 0
    assert SAMPLE_ROW_TILE % t_s == 0

    bucket_table = _bucket_of_distance(past + t_s)
    n_prompt_bias = _num_varying_tiles(t, bucket_table) + 1
    n_sample_var = _num_varying_tiles(tc, bucket_table) - 1
    n_sample_bias = n_sample_var + 1
    assert past // tc > n_sample_var

    lamv = jnp.stack([lambda_q1[0], lambda_k1[0], lambda_q2[0], lambda_k2[0]], axis=0)
    bias_p, bias_s, bias_n, lam = pl.pallas_call(
        functools.partial(_bias_kernel, n_prompt=n_prompt_bias, t=t, n_sample=n_sample_bias, tc=tc,
                          past=past, ts=t_s, bucket_table=bucket_table, lam_init=lam_init),
        in_specs=[pl.BlockSpec(memory_space=pltpu.SMEM), pl.BlockSpec(memory_space=pltpu.VMEM)],
        out_shape=[jax.ShapeDtypeStruct((N_HEADS, n_prompt_bias, t, t), F32),
                   jax.ShapeDtypeStruct((N_HEADS, n_sample_bias, t_s, tc), F32),
                   jax.ShapeDtypeStruct((N_HEADS, t_s, t_s), F32),
                   jax.ShapeDtypeStruct((8, 128), F32)],
        compiler_params=pltpu.CompilerParams(vmem_limit_bytes=VMEM_LIMIT),
        name="rel_bias_tiles",
    )(rel_bias_table, lamv)

    w_in_b = w_in[0].astype(BF16)
    lng = gmlp_ln_gain[0].reshape(1, D_GMLP)
    lnb = gmlp_ln_bias[0].reshape(1, D_GMLP)
    bsp = jnp.repeat(b_spatial[0].T, GROUP_DIM, axis=1)
    g1, b1 = ln1_gain[0].reshape(1, d_model), ln1_bias[0].reshape(1, d_model)
    g2, b2 = ln2_gain[0].reshape(1, d_model), ln2_bias[0].reshape(1, d_model)
    gain = subln_gain[0]

    xp = x_prompt.reshape(b_p * t_p, d_model)
    (q_p, k_p, v_p, kb_p, vt_p, gm_p, w_out_b, w_up_b, w_down_b) = _project(
        xp, w_in_b, lng, lnb, w_spatial[0], bsp, chunk=GMLP_CHUNK, att_tile=t, sample=False,
        cast_along=(w_out[0], w_ffn_up[0], w_ffn_down[0]))
    nq = t_p // t
    gain_t = jnp.broadcast_to(gain[:, :, None], (N_HEADS, D_V, t))
    attn_p = pl.pallas_call(
        functools.partial(_attn_prompt_kernel, t=t, n_bias=n_prompt_bias, lam_init=lam_init),
        grid=(nq, b_p + 1),
        in_specs=[pl.BlockSpec(memory_space=pltpu.SMEM), _const_spec(gain_t.shape),
                  pl.BlockSpec((t, D_ATT), lambda i, bb: (jnp.minimum(bb, b_p - 1) * nq + i, 0)),
                  pl.BlockSpec((t_p, D_ATT), lambda i, bb: (jnp.minimum(bb, b_p - 1), 0)),
                  pl.BlockSpec((nq, N_HEADS * VT_ROWS, t), lambda i, bb: (jnp.maximum(bb - 1, 0), 0, 0)),
                  _const_spec(bias_p.shape)],
        out_specs=pl.BlockSpec((t, D_ATT), lambda i, bb: (jnp.maximum(bb - 1, 0) * nq + i, 0)),
        out_shape=jax.ShapeDtypeStruct((b_p * t_p, D_ATT), BF16),
        scratch_shapes=[pltpu.VMEM((N_HEADS, 2 * t, D_V), BF16), pltpu.VMEM((N_HEADS, VT_ROWS, 2 * t), F32),
                        pltpu.VMEM((N_HEADS, 1, 2 * t), F32), pltpu.VMEM((N_HEADS, 1, 2 * t), F32),
                        pltpu.VMEM((N_HEADS, t_p, 2 * t), F32), pltpu.VMEM((N_HEADS, t_p, 2 * t), F32)],
        compiler_params=_params("arbitrary", "arbitrary"),
        name="attn_prompt",
    )(lam, gain_t, q_p, kb_p, vt_p, bias_p)

    xs = x_sample.reshape(b_s * t_s, d_model)
    q_s, k_s, v_s, kb_s, vb_s, gm_s, gn_s = _project(xs, w_in_b, lng, lnb, w_spatial[0], bsp[:t_s],
                                                     chunk=t_s, att_tile=t, sample=True)
    sample = dict(lam=lam, gain=gain, q=q_s, k_new=kb_s, v_new=vb_s,
                  cache_k=cache_k.reshape(b_s, past * N_HEADS, D_V),
                  cache_v=cache_v.reshape(b_s, past * N_HEADS, D_V), cache_tile=tc,
                  bias_cache=bias_s, bias_new=bias_n, n_varying=n_sample_var, lam_init=lam_init)
    y_p, attn_s = _finish(xp, attn_p, gm_p, w_out_b, g1, b1, w_up_b, w_down_b, g2, b2, alpha=alpha,
                          name="finish_prompt_attn_sample", sample=sample)
    y_s = _finish(xs, attn_s, gm_s, w_out_b, g1, b1, w_up_b, w_down_b, g2, b2, alpha=alpha,
                  name="finish_sample")

    head_shape_p = (depth, b_p, t_p, N_HEADS, D_V)
    head_shape_s = (depth, b_s, t_s, N_HEADS, D_V)
    return (y_p.reshape(b_p, t_p, d_model), y_s.reshape(b_s, t_s, d_model),
            k_p.reshape(head_shape_p), v_p.reshape(head_shape_p),
            k_s.reshape(head_shape_s), v_s.reshape(head_shape_s),
            gn_s.reshape(depth, b_s, t_s, N_GROUPS, GROUP_DIM))
```
